```python
import math
import jax
import jax.numpy as jnp
from jax import lax
import numpy as np

D_MODEL = 2048
BATCH = 1
SEQ = 8192
DEPTH = 2
DEC_BATCH = 32
DEC_SEQ = 4
PAST_LEN = 8192
PAGE_SIZE = 128

C_CONV = D_MODEL // 2
CONV_W = 31
H_ATTN = D_MODEL // 256
HD_ATTN = 128
C_ATTN = H_ATTN * HD_ATTN
MOBA_BLOCK = 256
MOBA_TOPK = 3
Q_BLOCK = 128
NUM_BUCKETS = 32
MAX_DISTANCE = 128
HS_RWKV = 64
H_RWKV = D_MODEL // HS_RWKV
LORA_W = 96
LORA_A = 96
LORA_G = 256
GN_EPS = 64e-5
D_FF = 4 * D_MODEL
N_AB = (DEPTH + 1) // 2
N_C = DEPTH // 2
ALPHA = (2 * DEPTH) ** 0.25
BETA = (8 * DEPTH) ** -0.25
LN_EPS = 1e-5

kernel_name = 'hybrid_conformer_moba_rwkv7_step'


def layer_norm(x, g, b, eps=LN_EPS):
    xf = x.astype(jnp.float32)
    mu = xf.mean(-1, keepdims=True)
    var = jnp.square(xf - mu).mean(-1, keepdims=True)
    return ((xf - mu) * lax.rsqrt(var + eps) * g + b).astype(x.dtype)


def sqrelu_mlp(x, w1, w2):
    return jnp.square(jax.nn.relu(x @ w1)) @ w2


def t5_bucket(dist):
    n = jnp.maximum(dist, 0)
    max_exact = NUM_BUCKETS // 2
    nf = jnp.maximum(n, 1).astype(jnp.float32)
    large = max_exact + (jnp.log(nf / max_exact) / math.log(MAX_DISTANCE / max_exact)
                         * (NUM_BUCKETS - max_exact)).astype(jnp.int32)
    large = jnp.minimum(large, NUM_BUCKETS - 1)
    return jnp.where(n < max_exact, n, large)


def moba_core(q, pos, kb, vb, kmean, rel_bias):
    B, Sq, H, HD = q.shape
    nb = kb.shape[2]
    qh = q.transpose(0, 2, 1, 3)
    own = pos // MOBA_BLOCK
    gate = jnp.einsum('bhqd,bhnd->bhqn', qh.astype(jnp.float32), kmean)
    is_past = jnp.arange(nb)[None, :] < own[:, None]
    gate = jnp.where(is_past, gate, -jnp.inf)
    _, sel = lax.top_k(gate, min(MOBA_TOPK, nb))
    sel_ok = sel < own[:, None]
    idx = jnp.concatenate([sel, jnp.broadcast_to(own[:, None], (B, H, Sq, 1))], axis=-1)
    blk_ok = jnp.concatenate([sel_ok, jnp.ones((B, H, Sq, 1), bool)], axis=-1)
    idx = jnp.minimum(idx, nb - 1)
    bi = jnp.arange(B)[:, None, None, None]
    hi = jnp.arange(H)[None, :, None, None]
    kg = kb[bi, hi, idx]
    vg = vb[bi, hi, idx]
    kpos = idx[..., None] * MOBA_BLOCK + jnp.arange(MOBA_BLOCK)
    dist = pos[:, None, None] - kpos
    keep = blk_ok[..., None] & (dist >= 0)
    bias = rel_bias.T[hi[..., None], t5_bucket(dist)].astype(jnp.float32)
    logits = jnp.einsum('bhqd,bhqikd->bhqik', qh, kg, preferred_element_type=jnp.float32) * (HD ** -0.5) + bias
    logits = jnp.where(keep, logits, -jnp.inf)
    K4 = idx.shape[-1]
    p = jax.nn.softmax(logits.reshape(B, H, Sq, K4 * MOBA_BLOCK), axis=-1).reshape(logits.shape)
    o = jnp.einsum('bhqik,bhqikd->bhqd', p.astype(vg.dtype), vg)
    return o.transpose(0, 2, 1, 3).reshape(B, Sq, H * HD)


def moba_attention(q, k_all, v_all, pos0, rel_bias):
    B, L, H, HD = q.shape
    T = k_all.shape[1]
    nb = -(-T // MOBA_BLOCK)
    pad = ((0, 0), (0, nb * MOBA_BLOCK - T), (0, 0), (0, 0))
    kb = jnp.pad(k_all, pad).reshape(B, nb, MOBA_BLOCK, H, HD).transpose(0, 3, 1, 2, 4)
    vb = jnp.pad(v_all, pad).reshape(B, nb, MOBA_BLOCK, H, HD).transpose(0, 3, 1, 2, 4)
    kmean = kb.astype(jnp.float32).mean(axis=3)
    if L <= Q_BLOCK:
        return moba_core(q, pos0 + jnp.arange(L), kb, vb, kmean, rel_bias)
    nc = -(-L // Q_BLOCK)
    lp = nc * Q_BLOCK
    qc = jnp.pad(q, ((0, 0), (0, lp - L), (0, 0), (0, 0))).reshape(B, nc, Q_BLOCK, H, HD).transpose(1, 0, 2, 3, 4)
    posc = (pos0 + jnp.arange(lp)).reshape(nc, Q_BLOCK)
    out = lax.map(lambda a: moba_core(a[0], a[1], kb, vb, kmean, rel_bias), (qc, posc))
    return out.transpose(1, 0, 2, 3).reshape(B, lp, H * HD)[:, :L]


def conv_attn_mixer(x, conv_prev, k_past, v_past, pos0, w_in, conv_w, conv_b, cln_g, cln_b, w_out, rel_bias):
    B, L, _ = x.shape
    h = x @ w_in
    a_val, a_gate, q, k, v = jnp.split(
        h, [C_CONV, 2 * C_CONV, 2 * C_CONV + C_ATTN, 2 * C_CONV + 2 * C_ATTN], axis=-1)
    u = a_val * jax.nn.sigmoid(a_gate)
    u_ext = jnp.concatenate([conv_prev.astype(u.dtype), u], axis=1)
    c = lax.conv_general_dilated(u_ext, conv_w[:, None, :].astype(u.dtype), (1,), 'VALID',
                                 dimension_numbers=('NWC', 'WIO', 'NWC'),
                                 feature_group_count=C_CONV) + conv_b
    c = jax.nn.silu(layer_norm(c, cln_g, cln_b))
    q = q.reshape(B, L, H_ATTN, HD_ATTN)
    k = k.reshape(B, L, H_ATTN, HD_ATTN)
    v = v.reshape(B, L, H_ATTN, HD_ATTN)
    k_all = jnp.concatenate([k_past.astype(k.dtype), k], axis=1)
    v_all = jnp.concatenate([v_past.astype(v.dtype), v], axis=1)
    o = moba_attention(q, k_all, v_all, pos0, rel_bias)
    y = jnp.concatenate([c, o.astype(c.dtype)], axis=-1) @ w_out
    return y, k, v, u_ext[:, -(CONV_W - 1):]


def rwkv7_mixer(x, shift_prev, wkv_prev, mix, wr, wk, wv, wo, w0, w1, w2, a0, a1, a2, g1, g2,
                k_k, k_a, r_k, lnx_g, lnx_b):
    B, L, D = x.shape
    f32 = jnp.float32
    x_prev = jnp.concatenate([shift_prev[:, None].astype(x.dtype), x[:, :-1]], axis=1)
    xmix = x[None] + (x_prev - x)[None] * mix[:, None, None, :]
    xr, xw, xk, xv, xa, xg = xmix[0], xmix[1], xmix[2], xmix[3], xmix[4], xmix[5]
    heads = lambda t: t.reshape(B, L, H_RWKV, HS_RWKV).astype(f32)
    r = heads(xr @ wr)
    k = heads(xk @ wk)
    v = heads(xv @ wv)
    w = heads(-jax.nn.softplus(-(w0 + jnp.tanh(xw @ w1) @ w2)) - 0.5)
    a = heads(jax.nn.sigmoid(a0 + (xa @ a1) @ a2))
    g = jax.nn.sigmoid(xg @ g1) @ g2
    kk = k * k_k.reshape(H_RWKV, HS_RWKV).astype(f32)
    kk = kk / jnp.maximum(jnp.sqrt(jnp.sum(kk * kk, -1, keepdims=True)), 1e-12)
    k = k * (1.0 + (a - 1.0) * k_a.reshape(H_RWKV, HS_RWKV).astype(f32))
    decay = jnp.exp(-jnp.exp(w))
    tmaj = lambda t: jnp.swapaxes(t, 0, 1)

    def step(S, inp):
        r_t, d_t, k_t, v_t, a_t, b_t = inp
        sa = jnp.einsum('bhvk,bhk->bhv', S, a_t)
        S = S * d_t[:, :, None, :] + sa[..., None] * b_t[:, :, None, :] + v_t[..., None] * k_t[:, :, None, :]
        return S, jnp.einsum('bhvk,bhk->bhv', S, r_t)

    S, y = lax.scan(step, wkv_prev.astype(f32),
                    (tmaj(r), tmaj(decay), tmaj(k), tmaj(v), tmaj(-kk), tmaj(kk * a)))
    y = tmaj(y)
    mu = y.mean(-1, keepdims=True)
    var = jnp.square(y - mu).mean(-1, keepdims=True)
    y = ((y - mu) * lax.rsqrt(var + GN_EPS)).reshape(B, L, D) * lnx_g + lnx_b
    y = y + (jnp.sum(r * k * r_k.astype(f32), -1, keepdims=True) * v).reshape(B, L, D)
    out = (y.astype(x.dtype) * g) @ wo
    return out, S.astype(wkv_prev.dtype), x[:, -1]


def setup_inputs(seed: int = 0) -> dict:
    key = jax.random.key(seed)
    ks = jax.random.split(key, 48)
    counter = [0]

    def nk():
        counter[0] += 1
        return ks[counter[0] - 1]

    f32 = jnp.float32

    def nrm(shape, scale):
        return jax.random.normal(nk(), shape, f32) * scale

    n_pages = PAST_LEN // PAGE_SIZE
    n_used = DEC_BATCH * n_pages
    n_pool = (5 * n_used + 3) // 4
    page_table = jax.random.permutation(nk(), n_pool)[:n_used].reshape(DEC_BATCH, n_pages).astype(jnp.int32)
    c_in = 2 * C_CONV + 3 * C_ATTN
    c_mix = C_CONV + C_ATTN
    D = D_MODEL
    return {
        'x_prompt': nrm((BATCH, SEQ, D), 1.0),
        'x_sample': nrm((DEC_BATCH, DEC_SEQ, D), 1.0),
        'cache_k': nrm((N_AB, n_pool, PAGE_SIZE, H_ATTN, HD_ATTN), 1.0),
        'cache_v': nrm((N_AB, n_pool, PAGE_SIZE, H_ATTN, HD_ATTN), 1.0),
        'page_table': page_table,
        'state_conv': nrm((N_AB, DEC_BATCH, CONV_W - 1, C_CONV), 0.5),
        'state_wkv': nrm((N_C, DEC_BATCH, H_RWKV, HS_RWKV, HS_RWKV), 0.3),
        'state_shift': nrm((N_C, DEC_BATCH, D), 1.0),
        'rel_bias': nrm((NUM_BUCKETS, H_ATTN), 0.5),
        'ab_w_in': nrm((N_AB, D, c_in), D ** -0.5),
        'ab_conv_w': nrm((N_AB, CONV_W, C_CONV), CONV_W ** -0.5),
        'ab_conv_b': nrm((N_AB, C_CONV), 0.02),
        'ab_ln_g': 1.0 + nrm((N_AB, C_CONV), 0.02),
        'ab_ln_b': nrm((N_AB, C_CONV), 0.02),
        'ab_w_out': nrm((N_AB, c_mix, D), BETA * c_mix ** -0.5),
        'rw_mix': jax.random.uniform(nk(), (N_C, 6, D), f32),
        'rw_wr': nrm((N_C, D, D), D ** -0.5),
        'rw_wk': nrm((N_C, D, D), D ** -0.5),
        'rw_wv': nrm((N_C, D, D), D ** -0.5),
        'rw_wo': nrm((N_C, D, D), BETA * D ** -0.5),
        'rw_w0': jax.random.uniform(nk(), (N_C, D), f32, -6.0, -1.0),
        'rw_w1': nrm((N_C, D, LORA_W), D ** -0.5),
        'rw_w2': nrm((N_C, LORA_W, D), 0.1 * LORA_W ** -0.5),
        'rw_a0': nrm((N_C, D), 0.1),
        'rw_a1': nrm((N_C, D, LORA_A), D ** -0.5),
        'rw_a2': nrm((N_C, LORA_A, D), 0.1 * LORA_A ** -0.5),
        'rw_g1': nrm((N_C, D, LORA_G), D ** -0.5),
        'rw_g2': nrm((N_C, LORA_G, D), LORA_G ** -0.5),
        'rw_kk': 0.85 + nrm((N_C, D), 0.02),
        'rw_ka': 1.0 + nrm((N_C, D), 0.02),
        'rw_rk': nrm((N_C, H_RWKV, HS_RWKV), 0.1),
        'rw_lnx_g': 1.0 + nrm((N_C, D), 0.02),
        'rw_lnx_b': nrm((N_C, D), 0.02),
        'ln_g': 1.0 + nrm((DEPTH, 2, D), 0.02),
        'ln_b': nrm((DEPTH, 2, D), 0.02),
        'mlp_w1': nrm((DEPTH, D, D_FF), D ** -0.5),
        'mlp_w2': nrm((DEPTH, D_FF, D), BETA * D_FF ** -0.5),
    }


def reference(x_prompt, x_sample, cache_k, cache_v, page_table, state_conv, state_wkv, state_shift,
              rel_bias, ab_w_in, ab_conv_w, ab_conv_b, ab_ln_g, ab_ln_b, ab_w_out,
              rw_mix, rw_wr, rw_wk, rw_wv, rw_wo, rw_w0, rw_w1, rw_w2, rw_a0, rw_a1, rw_a2,
              rw_g1, rw_g2, rw_kk, rw_ka, rw_rk, rw_lnx_g, rw_lnx_b,
              ln_g, ln_b, mlp_w1, mlp_w2):
    B = x_prompt.shape[0]
    DB = x_sample.shape[0]
    past = page_table.shape[1] * cache_k.shape[2]
    xp, xs = x_prompt, x_sample
    kp_l, vp_l, ks_l, vs_l, cp_l, cs_l = [], [], [], [], [], []
    wp_l, ws_l, hp_l, hs_l = [], [], [], []
    for layer in range(DEPTH):
        i = layer // 2
        if layer % 2 == 0:
            ab = (ab_w_in[i], ab_conv_w[i], ab_conv_b[i], ab_ln_g[i], ab_ln_b[i], ab_w_out[i], rel_bias)
            empty = jnp.zeros((B, 0, H_ATTN, HD_ATTN), xp.dtype)
            yp, kp, vp, cp = conv_attn_mixer(xp, jnp.zeros((B, CONV_W - 1, C_CONV), xp.dtype),
                                             empty, empty, 0, *ab)
            k_past = cache_k[i, page_table].reshape(DB, past, H_ATTN, HD_ATTN)
            v_past = cache_v[i, page_table].reshape(DB, past, H_ATTN, HD_ATTN)
            ys, ks, vs, cs = conv_attn_mixer(xs, state_conv[i], k_past, v_past, past, *ab)
            kp_l.append(kp); vp_l.append(vp); ks_l.append(ks); vs_l.append(vs)
            cp_l.append(cp); cs_l.append(cs)
        else:
            rw = (rw_mix[i], rw_wr[i], rw_wk[i], rw_wv[i], rw_wo[i], rw_w0[i], rw_w1[i], rw_w2[i],
                  rw_a0[i], rw_a1[i], rw_a2[i], rw_g1[i], rw_g2[i], rw_kk[i], rw_ka[i], rw_rk[i],
                  rw_lnx_g[i], rw_lnx_b[i])
            yp, wp, hp = rwkv7_mixer(xp, jnp.zeros((B, D_MODEL), xp.dtype),
                                     jnp.zeros((B, H_RWKV, HS_RWKV, HS_RWKV), xp.dtype), *rw)
            ys, ws, hs = rwkv7_mixer(xs, state_shift[i], state_wkv[i], *rw)
            wp_l.append(wp); ws_l.append(ws); hp_l.append(hp); hs_l.append(hs)
        xp = layer_norm(ALPHA * xp + yp, ln_g[layer, 0], ln_b[layer, 0])
        xs = layer_norm(ALPHA * xs + ys, ln_g[layer, 0], ln_b[layer, 0])
        xp = layer_norm(ALPHA * xp + sqrelu_mlp(xp, mlp_w1[layer], mlp_w2[layer]), ln_g[layer, 1], ln_b[layer, 1])
        xs = layer_norm(ALPHA * xs + sqrelu_mlp(xs, mlp_w1[layer], mlp_w2[layer]), ln_g[layer, 1], ln_b[layer, 1])
    k_prompt = jnp.stack(kp_l)
    v_prompt = jnp.stack(vp_l)
    k_sample = jnp.stack(ks_l)
    v_sample = jnp.stack(vs_l)
    conv_prompt = jnp.stack(cp_l)
    conv_sample = jnp.stack(cs_l)
    wkv_prompt = jnp.stack(wp_l)
    wkv_sample = jnp.stack(ws_l)
    shift_prompt = jnp.stack(hp_l)
    shift_sample = jnp.stack(hs_l)
    return (xp, xs, k_prompt, v_prompt, k_sample, v_sample, conv_prompt, conv_sample,
            wkv_prompt, wkv_sample, shift_prompt, shift_sample)
```

```python
import functools
import math

import jax
import jax.numpy as jnp
import numpy as np
from jax import lax
from jax.experimental import pallas as pl
from jax.experimental.pallas import tpu as pltpu

F32 = jnp.float32
BF16 = jnp.bfloat16
HIGHEST = lax.Precision.HIGHEST

LANES = 128
SUBLANES = 8
VMEM_LIMIT = 56 * 1024 * 1024

CONV_W = 31
HD_ATTN = 128
MOBA_BLOCK = 256
MOBA_TOPK = 3
NUM_BUCKETS = 32
MAX_DISTANCE = 128
HS_RWKV = 64
GN_EPS = 64e-5
LN_EPS = 1e-5
NEG = -1e30


def _params(sem):
    return pltpu.CompilerParams(dimension_semantics=sem, vmem_limit_bytes=VMEM_LIMIT)


def _pick(n, pref):
    if n <= pref:
        return n
    t = pref
    while n % t:
        t //= 2
    return t


def _softplus(y):
    return jnp.maximum(y, 0.0) + jnp.log(1.0 + jnp.exp(-jnp.abs(y)))


def _sigmoid(y):
    return 1.0 / (1.0 + jnp.exp(-y))


def _epilogue(acc, kind):
    if kind == "none":
        return acc
    if kind == "tanh":
        return jnp.tanh(acc)
    if kind == "sigmoid":
        return _sigmoid(acc)
    if kind == "decay":
        return jnp.exp(-jnp.exp(-_softplus(-acc) - 0.5))
    raise ValueError(kind)


def _mm_kernel(*refs, mix, bias, epilogue):
    it = iter(refs)
    x_ref = next(it)
    xp_ref = next(it) if mix else None
    m_ref = next(it) if mix else None
    w_ref = next(it)
    b_ref = next(it) if bias else None
    o_ref = next(it)
    xs_ref = next(it)

    @pl.when(pl.program_id(1) == 0)
    def _():
        x = x_ref[...]
        if mix:
            x = x + (xp_ref[...] - x) * m_ref[...]
        xs_ref[...] = x.astype(BF16)

    acc = jnp.dot(xs_ref[...], w_ref[...], preferred_element_type=F32)
    if bias:
        acc = acc + b_ref[...]
    o_ref[...] = _epilogue(acc, epilogue)


def matmul(x, w, *, xprev=None, mixrow=None, bias=None, epilogue="none", tm=512, tn=512):
    M, K = x.shape
    N = w.shape[1]
    tm = _pick(M, tm)
    tn = _pick(N, tn)
    mix = xprev is not None
    ins = [x]
    specs = [pl.BlockSpec((tm, K), lambda i, j: (i, 0))]
    if mix:
        ins += [xprev, mixrow]
        specs += [pl.BlockSpec((tm, K), lambda i, j: (i, 0)), pl.BlockSpec((1, K), lambda i, j: (0, 0))]
    ins.append(w)
    specs.append(pl.BlockSpec((K, tn), lambda i, j: (0, j)))
    if bias is not None:
        ins.append(bias)
        specs.append(pl.BlockSpec((1, tn), lambda i, j: (0, j)))
    return pl.pallas_call(
        functools.partial(_mm_kernel, mix=mix, bias=bias is not None, epilogue=epilogue),
        grid=(M // tm, N // tn),
        in_specs=specs,
        out_specs=pl.BlockSpec((tm, tn), lambda i, j: (i, j)),
        out_shape=jax.ShapeDtypeStruct((M, N), F32),
        scratch_shapes=[pltpu.VMEM((tm, K), BF16)],
        compiler_params=_params(("parallel", "arbitrary")),
    )(*ins)


def _layer_norm(z, g, b):
    mu = jnp.mean(z, axis=-1, keepdims=True)
    zc = z - mu
    var = jnp.mean(zc * zc, axis=-1, keepdims=True)
    return zc * lax.rsqrt(var + LN_EPS) * g + b


def _rowmm_ln_kernel(x_ref, w_ref, r_ref, g_ref, b_ref, o_ref, acc_ref, *, alpha):
    k = pl.program_id(1)

    @pl.when(k == 0)
    def _():
        acc_ref[...] = jnp.zeros_like(acc_ref)

    acc_ref[...] += jnp.dot(x_ref[...].astype(BF16), w_ref[...], preferred_element_type=F32)

    @pl.when(k == pl.num_programs(1) - 1)
    def _():
        o_ref[...] = _layer_norm(alpha * r_ref[...] + acc_ref[...], g_ref[...], b_ref[...])


def rowmm_ln(x, w, resid, g, b, alpha, *, tm=512, tk=512):
    M, K = x.shape
    N = w.shape[1]
    tm = _pick(M, tm)
    tk = _pick(K, tk)
    return pl.pallas_call(
        functools.partial(_rowmm_ln_kernel, alpha=alpha),
        grid=(M // tm, K // tk),
        in_specs=[
            pl.BlockSpec((tm, tk), lambda i, k: (i, k)),
            pl.BlockSpec((tk, N), lambda i, k: (k, 0)),
            pl.BlockSpec((tm, N), lambda i, k: (i, 0)),
            pl.BlockSpec((1, N), lambda i, k: (0, 0)),
            pl.BlockSpec((1, N), lambda i, k: (0, 0)),
        ],
        out_specs=pl.BlockSpec((tm, N), lambda i, k: (i, 0)),
        out_shape=jax.ShapeDtypeStruct((M, N), F32),
        scratch_shapes=[pltpu.VMEM((tm, N), F32)],
        compiler_params=_params(("parallel", "arbitrary")),
    )(x, w, resid, g, b)


def _mlp_ln_kernel(x_ref, w1_ref, w2_ref, g_ref, b_ref, o_ref, xs_ref, acc_ref, *, alpha):
    f = pl.program_id(1)

    @pl.when(f == 0)
    def _():
        xs_ref[...] = x_ref[...].astype(BF16)
        acc_ref[...] = jnp.zeros_like(acc_ref)

    h = jnp.dot(xs_ref[...], w1_ref[...], preferred_element_type=F32)
    h = jnp.maximum(h, 0.0)
    h = (h * h).astype(BF16)
    acc_ref[...] += jnp.dot(h, w2_ref[...], preferred_element_type=F32)

    @pl.when(f == pl.num_programs(1) - 1)
    def _():
        o_ref[...] = _layer_norm(alpha * x_ref[...] + acc_ref[...], g_ref[...], b_ref[...])


def mlp_ln(x, w1, w2, g, b, alpha, *, tm=512, tf=512):
    M, D = x.shape
    FF = w1.shape[1]
    tm = _pick(M, tm)
    tf = _pick(FF, tf)
    return pl.pallas_call(
        functools.partial(_mlp_ln_kernel, alpha=alpha),
        grid=(M // tm, FF // tf),
        in_specs=[
            pl.BlockSpec((tm, D), lambda i, f: (i, 0)),
            pl.BlockSpec((D, tf), lambda i, f: (0, f)),
            pl.BlockSpec((tf, D), lambda i, f: (f, 0)),
            pl.BlockSpec((1, D), lambda i, f: (0, 0)),
            pl.BlockSpec((1, D), lambda i, f: (0, 0)),
        ],
        out_specs=pl.BlockSpec((tm, D), lambda i, f: (i, 0)),
        out_shape=jax.ShapeDtypeStruct((M, D), F32),
        scratch_shapes=[pltpu.VMEM((tm, D), BF16), pltpu.VMEM((tm, D), F32)],
        compiler_params=_params(("parallel", "arbitrary")),
    )(x, w1, w2, g, b)


HALO = 32


def _conv_kernel(hv_ref, hg_ref, prev_ref, w_ref, cb_ref, g_ref, b_ref, c_ref, st_ref, ue_ref, *, tl):
    l = pl.program_id(1)
    npv = CONV_W - 1
    off = HALO - npv

    @pl.when(l == 0)
    def _():
        ue_ref[pl.ds(off, npv), :] = prev_ref[...]

    u = hv_ref[...] * _sigmoid(hg_ref[...])
    ue_ref[pl.ds(HALO, tl), :] = u
    acc = jnp.zeros_like(u) + cb_ref[...]
    for j in range(CONV_W):
        acc = acc + ue_ref[pl.ds(off + j, tl), :] * w_ref[pl.ds(j, 1), :]
    y = _layer_norm(acc, g_ref[...], b_ref[...])
    c_ref[...] = y * _sigmoid(y)
    tail = ue_ref[pl.ds(off + tl, npv), :]
    ue_ref[pl.ds(off, npv), :] = tail

    @pl.when(l == pl.num_programs(1) - 1)
    def _():
        st_ref[...] = tail


def conv_module(h3, conv_prev, conv_w, conv_b, ln_g, ln_b, c_conv, *, tl=256):
    B, L, _ = h3.shape
    tl = _pick(L, tl)
    npv = CONV_W - 1
    row = lambda a: a.reshape(1, c_conv)
    return pl.pallas_call(
        functools.partial(_conv_kernel, tl=tl),
        grid=(B, L // tl),
        in_specs=[
            pl.BlockSpec((None, tl, c_conv), lambda b, l: (b, l, 0)),
            pl.BlockSpec((None, tl, c_conv), lambda b, l: (b, l, 1)),
            pl.BlockSpec((None, npv, c_conv), lambda b, l: (b, 0, 0)),
            pl.BlockSpec((CONV_W, c_conv), lambda b, l: (0, 0)),
            pl.BlockSpec((1, c_conv), lambda b, l: (0, 0)),
            pl.BlockSpec((1, c_conv), lambda b, l: (0, 0)),
            pl.BlockSpec((1, c_conv), lambda b, l: (0, 0)),
        ],
        out_specs=[
            pl.BlockSpec((None, tl, c_conv), lambda b, l: (b, l, 0)),
            pl.BlockSpec((None, npv, c_conv), lambda b, l: (b, 0, 0)),
        ],
        out_shape=[
            jax.ShapeDtypeStruct((B, L, c_conv), F32),
            jax.ShapeDtypeStruct((B, npv, c_conv), F32),
        ],
        scratch_shapes=[pltpu.VMEM((HALO + tl, c_conv), F32)],
        compiler_params=_params(("arbitrary", "arbitrary")),
    )(h3, h3, conv_prev, conv_w, row(conv_b), row(ln_g), row(ln_b))


def _t5_bucket(dist):
    n = jnp.maximum(dist, 0)
    max_exact = NUM_BUCKETS // 2
    nf = jnp.maximum(n, 1).astype(F32)
    large = max_exact + (jnp.log(nf / max_exact) / math.log(MAX_DISTANCE / max_exact)
                         * (NUM_BUCKETS - max_exact)).astype(jnp.int32)
    large = jnp.minimum(large, NUM_BUCKETS - 1)
    return jnp.where(n < max_exact, n, large)


def _bias_of_dist(rel_bias, dist):
    b = jnp.moveaxis(rel_bias[_t5_bucket(dist)], -1, 0)
    return jnp.where(dist[None] >= 0, b, NEG)


def _kmean_kernel(k_ref, o_ref, *, nb):
    k = k_ref[...]
    km = jnp.mean(k.reshape(nb, MOBA_BLOCK, HD_ATTN), axis=1)
    o_ref[...] = jnp.zeros_like(o_ref)
    o_ref[pl.ds(0, nb), :] = km


def _select_topk(gate, valid, axis):
    idx = lax.broadcasted_iota(jnp.int32, gate.shape, axis).astype(F32)
    big = 1e9
    g = jnp.where(valid, gate, -jnp.inf)
    sel = jnp.zeros(gate.shape, jnp.bool_)
    for _ in range(MOBA_TOPK):
        mx = jnp.max(g, axis=axis, keepdims=True)
        is_max = (g == mx) & (mx > -jnp.inf)
        first = jnp.min(jnp.where(is_max, idx, big), axis=axis, keepdims=True)
        pick = idx == first
        sel = sel | pick
        g = jnp.where(pick, -jnp.inf, g)
    return jnp.where(sel, 0.0, NEG).astype(F32)


def _flash_kernel(ia_ref, ja_ref, q_ref, k_ref, v_ref, km_ref, bd_ref, bp_ref, bf_ref, o_ref,
                  qs_ref, m_ref, l_ref, acc_ref, sel_ref):
    s_idx = pl.program_id(1)
    i = ia_ref[s_idx]
    j = ja_ref[s_idx]
    scale = HD_ATTN ** -0.5

    @pl.when(j == 0)
    def _():
        q = q_ref[...]
        qs_ref[...] = (q * scale).astype(BF16)
        gate = lax.dot_general(q, km_ref[...], (((1,), (1,)), ((), ())),
                               precision=HIGHEST, preferred_element_type=F32)
        lane = lax.broadcasted_iota(jnp.int32, gate.shape, 1)
        sel_ref[...] = _select_topk(gate, lane < i, 1)
        m_ref[...] = jnp.full_like(m_ref, NEG)
        l_ref[...] = jnp.zeros_like(l_ref)
        acc_ref[...] = jnp.zeros_like(acc_ref)

    def update(extra):
        s = lax.dot_general(qs_ref[...], k_ref[...].astype(BF16), (((1,), (1,)), ((), ())),
                            preferred_element_type=F32)
        s = s + extra
        m_old = m_ref[...]
        m_new = jnp.maximum(m_old, jnp.max(s, axis=-1, keepdims=True))
        alpha = jnp.exp(m_old - m_new)
        p = jnp.exp(s - m_new)
        l_ref[...] = alpha * l_ref[...] + jnp.sum(p, axis=-1, keepdims=True)
        acc_ref[...] = alpha * acc_ref[...] + jnp.dot(p.astype(BF16), v_ref[...].astype(BF16),
                                                      preferred_element_type=F32)
        m_ref[...] = m_new

    def colmask():
        lane = lax.broadcasted_iota(jnp.int32, sel_ref.shape, 1)
        return jnp.sum(jnp.where(lane == j, sel_ref[...], 0.0), axis=-1, keepdims=True)

    @pl.when(j < i - 1)
    def _():
        update(bf_ref[...] + colmask())

    @pl.when(j == i - 1)
    def _():
        update(bp_ref[...] + colmask())

    @pl.when(j == i)
    def _():
        update(bd_ref[...])
        o_ref[...] = acc_ref[...] / l_ref[...]


def moba_prompt(h2, rel_bias, n_heads, q_col, k_col, v_col):
    L = h2.shape[0]
    nb = L // MOBA_BLOCK
    assert L % MOBA_BLOCK == 0 and nb <= LANES
    T = MOBA_BLOCK
    kmean = pl.pallas_call(
        functools.partial(_kmean_kernel, nb=nb),
        grid=(n_heads,),
        in_specs=[pl.BlockSpec((L, HD_ATTN), lambda h: (0, k_col + h))],
        out_specs=pl.BlockSpec((None, LANES, HD_ATTN), lambda h: (h, 0, 0)),
        out_shape=jax.ShapeDtypeStruct((n_heads, LANES, HD_ATTN), F32),
        compiler_params=_params(("parallel",)),
    )(h2)

    qi = jnp.arange(T)[:, None]
    ki = jnp.arange(T)[None, :]
    bias_diag = _bias_of_dist(rel_bias, qi - ki)
    bias_prev = _bias_of_dist(rel_bias, qi - ki + T)
    bias_far = jnp.broadcast_to(rel_bias[NUM_BUCKETS - 1][:, None, None], (n_heads, 1, T))

    ii, jj = np.tril_indices(nb)
    ia = jnp.asarray(ii, jnp.int32)
    ja = jnp.asarray(jj, jnp.int32)
    grid_spec = pltpu.PrefetchScalarGridSpec(
        num_scalar_prefetch=2,
        grid=(n_heads, len(ii)),
        in_specs=[
            pl.BlockSpec((T, HD_ATTN), lambda h, s, ia, ja: (ia[s], q_col + h)),
            pl.BlockSpec((T, HD_ATTN), lambda h, s, ia, ja: (ja[s], k_col + h)),
            pl.BlockSpec((T, HD_ATTN), lambda h, s, ia, ja: (ja[s], v_col + h)),
            pl.BlockSpec((None, LANES, HD_ATTN), lambda h, s, ia, ja: (h, 0, 0)),
            pl.BlockSpec((None, T, T), lambda h, s, ia, ja: (h, 0, 0)),
            pl.BlockSpec((None, T, T), lambda h, s, ia, ja: (h, 0, 0)),
            pl.BlockSpec((None, 1, T), lambda h, s, ia, ja: (h, 0, 0)),
        ],
        out_specs=pl.BlockSpec((T, HD_ATTN), lambda h, s, ia, ja: (ia[s], h)),
        scratch_shapes=[
            pltpu.VMEM((T, HD_ATTN), BF16),
            pltpu.VMEM((T, 1), F32),
            pltpu.VMEM((T, 1), F32),
            pltpu.VMEM((T, HD_ATTN), F32),
            pltpu.VMEM((T, LANES), F32),
        ],
    )
    return pl.pallas_call(
        _flash_kernel,
        grid_spec=grid_spec,
        out_shape=jax.ShapeDtypeStruct((L, n_heads * HD_ATTN), F32),
        compiler_params=_params(("parallel", "arbitrary")),
    )(ia, ja, h2, h2, h2, kmean, bias_diag, bias_prev, bias_far)


QPAD = 8


def _sample_p1_kernel(pt_ref, k0_ref, k1_ref, w_ref, lg_ref, gate_ref):
    n = pl.program_id(1)
    w = w_ref[...]
    P = k0_ref.shape[0]
    ksum = jnp.zeros((1, k0_ref.shape[1]), F32)
    for half, ref in enumerate((k0_ref, k1_ref)):
        kb = ref[...]
        ksum = ksum + jnp.sum(kb, axis=0, keepdims=True)
        lg_ref[pl.ds(half * P, P), :] = jnp.dot(kb.astype(BF16), w.astype(BF16),
                                                preferred_element_type=F32) * (HD_ATTN ** -0.5)
    kmean = jnp.broadcast_to(ksum * (1.0 / MOBA_BLOCK), (SUBLANES, ksum.shape[1]))
    g = jnp.dot(kmean, w, precision=HIGHEST, preferred_element_type=F32)
    gate_ref[pl.ds(n, 1), :] = g[0:1]


def _sample_p2_kernel(lg_ref, gate_ref, kn_ref, w_ref, bl_ref, bf_ref, bo_ref, p_ref, pn_ref, *, nbp):
    gate = gate_ref[...]
    selm = _select_topk(gate, jnp.ones(gate.shape, jnp.bool_), 0)
    s = lg_ref[...] + selm[:, None, :]
    blk = lax.broadcasted_iota(jnp.int32, (nbp, 1, 1), 0)
    s = s + jnp.where(blk == nbp - 1, bl_ref[...][None], bf_ref[...][None])
    w = w_ref[...]
    sn = jnp.dot(kn_ref[...].astype(BF16), w.astype(BF16), preferred_element_type=F32) * (HD_ATTN ** -0.5)
    sn = sn + bo_ref[...]
    m = jnp.maximum(jnp.max(jnp.max(s, axis=1), axis=0, keepdims=True), jnp.max(sn, axis=0, keepdims=True))
    p = jnp.exp(s - m[None])
    pn = jnp.exp(sn - m)
    den = jnp.sum(jnp.sum(p, axis=1), axis=0, keepdims=True) + jnp.sum(pn, axis=0, keepdims=True)
    inv = 1.0 / den
    p_ref[...] = p * inv[None]
    pn_ref[...] = pn * inv


def _sample_p3_kernel(pt_ref, v0_ref, v1_ref, p_ref, pn_ref, vn_ref, o_ref, acc_ref, *, n_heads):
    n = pl.program_id(1)
    P = v0_ref.shape[0]

    @pl.when(n == 0)
    def _():
        acc_ref[...] = lax.dot_general(pn_ref[...].astype(BF16), vn_ref[...].astype(BF16),
                                       (((0,), (0,)), ((), ())), preferred_element_type=F32)

    for half, ref in enumerate((v0_ref, v1_ref)):
        p = p_ref[pl.ds(half * P, P), :].astype(BF16)
        acc_ref[...] += lax.dot_general(p, ref[...].astype(BF16), (((0,), (0,)), ((), ())),
                                        preferred_element_type=F32)

    @pl.when(n == pl.num_programs(1) - 1)
    def _():
        for h in range(n_heads):
            o_ref[:, pl.ds(h * HD_ATTN, HD_ATTN)] = acc_ref[pl.ds(h * QPAD, QPAD), pl.ds(h * HD_ATTN, HD_ATTN)]


def moba_sample(q, k_new, v_new, cache_k, cache_v, page_table, rel_bias):
    B, L, H, HD = q.shape
    page = cache_k.shape[1]
    n_pages = page_table.shape[1]
    past = n_pages * page
    assert MOBA_BLOCK == 2 * page and past % MOBA_BLOCK == 0 and L <= QPAD
    nbp = past // MOBA_BLOCK
    C = H * QPAD
    CH = H * HD
    ck = cache_k.reshape(cache_k.shape[0], page, CH)
    cv = cache_v.reshape(cache_v.shape[0], page, CH)

    qp = jnp.pad(q, ((0, 0), (0, QPAD - L), (0, 0), (0, 0)))
    eye = jnp.eye(H, dtype=F32)
    wbd = jnp.einsum("bqhd,hg->bhdgq", qp, eye).reshape(B, CH, C)

    qpos = past + jnp.arange(QPAD)
    kl = (nbp - 1) * MOBA_BLOCK + jnp.arange(MOBA_BLOCK)
    bias_last = _bias_of_dist(rel_bias, qpos[None, :] - kl[:, None])
    bias_last = jnp.moveaxis(bias_last, 0, 1).reshape(MOBA_BLOCK, C)
    bias_far = jnp.repeat(rel_bias[NUM_BUCKETS - 1], QPAD).reshape(1, C)
    kown = past + jnp.arange(QPAD)
    dist_own = qpos[None, :] - kown[:, None]
    ok = (jnp.arange(QPAD)[:, None] < L) & (jnp.arange(QPAD)[None, :] < L)
    bias_own = _bias_of_dist(rel_bias, jnp.where(ok, dist_own, -1))
    bias_own = jnp.moveaxis(bias_own, 0, 1).reshape(QPAD, C)

    knp = jnp.pad(k_new.reshape(B, L, CH), ((0, 0), (0, QPAD - L), (0, 0)))
    vnp = jnp.pad(v_new.reshape(B, L, CH), ((0, 0), (0, QPAD - L), (0, 0)))

    page_spec = lambda half: pl.BlockSpec((None, page, CH), lambda b, n, pt: (pt[b, 2 * n + half], 0, 0))
    logits, gate = pl.pallas_call(
        _sample_p1_kernel,
        grid_spec=pltpu.PrefetchScalarGridSpec(
            num_scalar_prefetch=1,
            grid=(B, nbp),
            in_specs=[page_spec(0), page_spec(1),
                      pl.BlockSpec((None, CH, C), lambda b, n, pt: (b, 0, 0))],
            out_specs=[pl.BlockSpec((None, None, MOBA_BLOCK, C), lambda b, n, pt: (b, n, 0, 0)),
                       pl.BlockSpec((None, nbp, C), lambda b, n, pt: (b, 0, 0))],
        ),
        out_shape=[jax.ShapeDtypeStruct((B, nbp, MOBA_BLOCK, C), F32),
                   jax.ShapeDtypeStruct((B, nbp, C), F32)],
        compiler_params=_params(("parallel", "arbitrary")),
    )(page_table, ck, ck, wbd)

    p_past, p_new = pl.pallas_call(
        functools.partial(_sample_p2_kernel, nbp=nbp),
        grid=(B,),
        in_specs=[
            pl.BlockSpec((None, nbp, MOBA_BLOCK, C), lambda b: (b, 0, 0, 0)),
            pl.BlockSpec((None, nbp, C), lambda b: (b, 0, 0)),
            pl.BlockSpec((None, QPAD, CH), lambda b: (b, 0, 0)),
            pl.BlockSpec((None, CH, C), lambda b: (b, 0, 0)),
            pl.BlockSpec((MOBA_BLOCK, C), lambda b: (0, 0)),
            pl.BlockSpec((1, C), lambda b: (0, 0)),
            pl.BlockSpec((QPAD, C), lambda b: (0, 0)),
        ],
        out_specs=[pl.BlockSpec((None, nbp, MOBA_BLOCK, C), lambda b: (b, 0, 0, 0)),
                   pl.BlockSpec((None, QPAD, C), lambda b: (b, 0, 0))],
        out_shape=[jax.ShapeDtypeStruct((B, nbp, MOBA_BLOCK, C), F32),
                   jax.ShapeDtypeStruct((B, QPAD, C), F32)],
        compiler_params=_params(("parallel",)),
    )(logits, gate, knp, wbd, bias_last, bias_far, bias_own)

    o = pl.pallas_call(
        functools.partial(_sample_p3_kernel, n_heads=H),
        grid_spec=pltpu.PrefetchScalarGridSpec(
            num_scalar_prefetch=1,
            grid=(B, nbp),
            in_specs=[page_spec(0), page_spec(1),
                      pl.BlockSpec((None, None, MOBA_BLOCK, C), lambda b, n, pt: (b, n, 0, 0)),
                      pl.BlockSpec((None, QPAD, C), lambda b, n, pt: (b, 0, 0)),
                      pl.BlockSpec((None, QPAD, CH), lambda b, n, pt: (b, 0, 0))],
            out_specs=pl.BlockSpec((None, QPAD, CH), lambda b, n, pt: (b, 0, 0)),
            scratch_shapes=[pltpu.VMEM((C, CH), F32)],
        ),
        out_shape=jax.ShapeDtypeStruct((B, QPAD, CH), F32),
        compiler_params=_params(("parallel", "arbitrary")),
    )(page_table, cv, cv, p_past, p_new, vnp)
    return o[:, :L]


def _seg_ones():
    r = lax.broadcasted_iota(jnp.int32, (LANES, LANES), 0) // HS_RWKV
    c = lax.broadcasted_iota(jnp.int32, (LANES, LANES), 1) // HS_RWKV
    return (r == c).astype(F32)


def _head_sum(x):
    ones = _seg_ones()
    parts = [jnp.dot(x[:, c:c + LANES], ones, precision=HIGHEST, preferred_element_type=F32)
             for c in range(0, x.shape[1], LANES)]
    return jnp.concatenate(parts, axis=1)


def _rwkv_prep_kernel(k_ref, a_ref, kk_ref, ka_ref, k2_ref, na_ref, nb_ref):
    k = k_ref[...]
    a = a_ref[...]
    kk = k * kk_ref[...]
    nrm = jnp.maximum(jnp.sqrt(_head_sum(kk * kk)), 1e-12)
    kk = kk / nrm
    k2_ref[...] = k * (1.0 + (a - 1.0) * ka_ref[...])
    na_ref[...] = -kk
    nb_ref[...] = kk * a


def rwkv_prep(k, a, k_k, k_a, *, tm=256):
    M, D = k.shape
    tm = _pick(M, tm)
    blk = pl.BlockSpec((tm, D), lambda i: (i, 0))
    row = pl.BlockSpec((1, D), lambda i: (0, 0))
    return pl.pallas_call(
        _rwkv_prep_kernel,
        grid=(M // tm,),
        in_specs=[blk, blk, row, row],
        out_specs=[blk, blk, blk],
        out_shape=[jax.ShapeDtypeStruct((M, D), F32)] * 3,
        compiler_params=_params(("parallel",)),
    )(k, a, k_k.reshape(1, D), k_a.reshape(1, D))


def _rwkv_post_kernel(y_ref, r_ref, k2_ref, v_ref, g_ref, rk_ref, lg_ref, lb_ref, o_ref):
    y = y_ref[...]
    inv = 1.0 / HS_RWKV
    mu = _head_sum(y) * inv
    yc = y - mu
    var = _head_sum(yc * yc) * inv
    yn = yc * lax.rsqrt(var + GN_EPS) * lg_ref[...] + lb_ref[...]
    bonus = _head_sum(r_ref[...] * k2_ref[...] * rk_ref[...]) * v_ref[...]
    o_ref[...] = (yn + bonus) * g_ref[...]


def rwkv_post(y, r, k2, v, g, r_k, lnx_g, lnx_b, *, tm=256):
    M, D = y.shape
    tm = _pick(M, tm)
    blk = pl.BlockSpec((tm, D), lambda i: (i, 0))
    row = pl.BlockSpec((1, D), lambda i: (0, 0))
    return pl.pallas_call(
        _rwkv_post_kernel,
        grid=(M // tm,),
        in_specs=[blk] * 5 + [row] * 3,
        out_specs=blk,
        out_shape=jax.ShapeDtypeStruct((M, D), F32),
        compiler_params=_params(("parallel",)),
    )(y, r, k2, v, g, r_k.reshape(1, D), lnx_g.reshape(1, D), lnx_b.reshape(1, D))


def _scan_kernel(r_ref, d_ref, k_ref, a_ref, b_ref, v_ref, s0_ref, y_ref, so_ref, col_ref, s_ref, *, tc, vlo, nvh):
    c = pl.program_id(1)
    nh = r_ref.shape[-1]

    @pl.when(c == 0)
    def _():
        s_ref[...] = s0_ref[...]

    hh = lax.broadcasted_iota(jnp.int32, (nh, LANES), 0)
    ll = lax.broadcasted_iota(jnp.int32, (nh, LANES), 1) % nh
    tile = (hh == ll).astype(F32)
    for o, ref in enumerate((r_ref, d_ref, k_ref, a_ref, b_ref)):
        x = ref[...].reshape(tc * HS_RWKV, nh)
        col_ref[o] = jnp.dot(x, tile, precision=HIGHEST, preferred_element_type=F32).reshape(tc, HS_RWKV, LANES)

    def step(t, carry):
        r_c = col_ref[0, t]
        d_c = col_ref[1, t]
        k_c = col_ref[2, t]
        a_c = col_ref[3, t]
        b_c = col_ref[4, t]
        vt = v_ref[t]
        rows = []
        for vh in range(nvh):
            S = s_ref[vh]
            sa = jnp.sum(S * a_c, axis=0, keepdims=True)
            S = S * d_c + sa * b_c + vt[vh:vh + 1, :] * k_c
            s_ref[vh] = S
            rows.append(jnp.sum(S * r_c, axis=0, keepdims=True))
        y_ref[t] = jnp.concatenate(rows, axis=0)
        return carry

    lax.fori_loop(0, tc, step, 0)

    @pl.when(c == pl.num_programs(1) - 1)
    def _():
        so_ref[...] = s_ref[...]


def rwkv_scan(r, d, k, a, b, v, state, n_heads, *, tc=32):
    B, L, D = r.shape
    H = n_heads
    vlo = LANES // H
    nvh = HS_RWKV // vlo
    assert H * vlo == LANES and nvh * vlo == HS_RWKV
    tc = _pick(L, tc)
    kmaj = lambda x: x.reshape(B, L, H, HS_RWKV).transpose(0, 1, 3, 2)
    vperm = lambda x: x.reshape(B, L, H, nvh, vlo).transpose(0, 1, 3, 4, 2).reshape(B, L, nvh, LANES)
    s0 = state.reshape(B, H, nvh, vlo, HS_RWKV).transpose(0, 2, 4, 3, 1).reshape(B, nvh, HS_RWKV, LANES)
    kspec = pl.BlockSpec((None, tc, HS_RWKV, H), lambda bb, c: (bb, c, 0, 0))
    vspec = pl.BlockSpec((None, tc, nvh, LANES), lambda bb, c: (bb, c, 0, 0))
    sspec = pl.BlockSpec((None, nvh, HS_RWKV, LANES), lambda bb, c: (bb, 0, 0, 0))
    y, s1 = pl.pallas_call(
        functools.partial(_scan_kernel, tc=tc, vlo=vlo, nvh=nvh),
        grid=(B, L // tc),
        in_specs=[kspec] * 5 + [vspec, sspec],
        out_specs=[vspec, sspec],
        out_shape=[jax.ShapeDtypeStruct((B, L, nvh, LANES), F32),
                   jax.ShapeDtypeStruct((B, nvh, HS_RWKV, LANES), F32)],
        scratch_shapes=[pltpu.VMEM((5, tc, HS_RWKV, LANES), F32),
                        pltpu.VMEM((nvh, HS_RWKV, LANES), F32)],
        compiler_params=_params(("parallel", "arbitrary")),
    )(kmaj(r), kmaj(d), kmaj(k), kmaj(a), kmaj(b), vperm(v), s0)
    y = y.reshape(B, L, nvh, vlo, H).transpose(0, 1, 4, 2, 3).reshape(B, L, D)
    s1 = s1.reshape(B, nvh, HS_RWKV, vlo, H).transpose(0, 4, 1, 3, 2).reshape(B, H, HS_RWKV, HS_RWKV)
    return y, s1


def _conv_attn_layer(x, conv_prev, attn_fn, w_in, conv_w, conv_b, cln_g, cln_b, w_out, ln_g, ln_b, alpha,
                     c_conv, c_attn):
    B, L, D = x.shape
    x2 = x.reshape(B * L, D)
    h = matmul(x2, w_in).reshape(B, L, -1)
    c, cstate = conv_module(h, conv_prev, conv_w, conv_b, cln_g, cln_b, c_conv)
    o = attn_fn(h)
    co = jnp.concatenate([c, o], axis=-1).reshape(B * L, c_conv + c_attn)
    xn = rowmm_ln(co, w_out, x2, ln_g, ln_b, alpha).reshape(B, L, D)
    k = h[..., 2 * c_conv + c_attn:2 * c_conv + 2 * c_attn]
    v = h[..., 2 * c_conv + 2 * c_attn:]
    return xn, k, v, cstate


def _rwkv_layer(x, shift_prev, wkv_prev, p, ln_g, ln_b, alpha, n_heads):
    B, L, D = x.shape
    x_prev = jnp.concatenate([shift_prev[:, None], x[:, :-1]], axis=1).reshape(B * L, D)
    x2 = x.reshape(B * L, D)
    mixrow = lambda i: p["mix"][i].reshape(1, D)
    proj = lambda i, w, **kw: matmul(x2, w, xprev=x_prev, mixrow=mixrow(i), **kw)
    r = proj(0, p["wr"])
    lw = proj(1, p["w1"], epilogue="tanh")
    d = matmul(lw, p["w2"], bias=p["w0"].reshape(1, D), epilogue="decay")
    k = proj(2, p["wk"])
    v = proj(3, p["wv"])
    la = proj(4, p["a1"])
    a = matmul(la, p["a2"], bias=p["a0"].reshape(1, D), epilogue="sigmoid")
    lg = proj(5, p["g1"], epilogue="sigmoid")
    g = matmul(lg, p["g2"])
    k2, na, nb = rwkv_prep(k, a, p["kk"], p["ka"])
    sh = lambda t: t.reshape(B, L, D)
    y, s1 = rwkv_scan(sh(r), sh(d), sh(k2), sh(na), sh(nb), sh(v), wkv_prev, n_heads)
    yg = rwkv_post(y.reshape(B * L, D), r, k2, v, g, p["rk"], p["lnx_g"], p["lnx_b"])
    xn = rowmm_ln(yg, p["wo"], x2, ln_g, ln_b, alpha).reshape(B, L, D)
    return xn, s1, x[:, -1]


def kernel(x_prompt, x_sample, cache_k, cache_v, page_table, state_conv, state_wkv, state_shift, rel_bias, ab_w_in, ab_conv_w, ab_conv_b, ab_ln_g, ab_ln_b, ab_w_out, rw_mix, rw_wr, rw_wk, rw_wv, rw_wo, rw_w0, rw_w1, rw_w2, rw_a0, rw_a1, rw_a2, rw_g1, rw_g2, rw_kk, rw_ka, rw_rk, rw_lnx_g, rw_lnx_b, ln_g, ln_b, mlp_w1, mlp_w2):
    B, L, D = x_prompt.shape
    DB, DL, _ = x_sample.shape
    depth = ln_g.shape[0]
    alpha = (2 * depth) ** 0.25
    c_conv = ab_conv_w.shape[2]
    n_heads_attn = cache_k.shape[3]
    c_attn = n_heads_attn * HD_ATTN
    n_heads_rwkv = D // HS_RWKV
    assert B == 1
    bf = lambda w: w.astype(BF16)
    row = lambda a: a.reshape(1, -1)
    q_col = 2 * c_conv // LANES
    k_col = q_col + c_attn // LANES
    v_col = k_col + c_attn // LANES

    xp, xs = x_prompt, x_sample
    outs = {n: [] for n in ("kp", "vp", "ks", "vs", "cp", "cs", "wp", "ws", "hp", "hs")}
    for layer in range(depth):
        i = layer // 2
        g0, b0, g1, b1 = row(ln_g[layer, 0]), row(ln_b[layer, 0]), row(ln_g[layer, 1]), row(ln_b[layer, 1])
        if layer % 2 == 0:
            ab = (bf(ab_w_in[i]), ab_conv_w[i], ab_conv_b[i], ab_ln_g[i], ab_ln_b[i], bf(ab_w_out[i]), g0, b0, alpha,
                  c_conv, c_attn)
            attn_p = lambda h: moba_prompt(h.reshape(L, -1), rel_bias, n_heads_attn, q_col, k_col, v_col)[None]
            xp, kp, vp, cp = _conv_attn_layer(xp, jnp.zeros((B, CONV_W - 1, c_conv), F32), attn_p, *ab)

            def attn_s(h):
                hd = lambda lo: h[..., lo:lo + c_attn].reshape(DB, DL, n_heads_attn, HD_ATTN)
                return moba_sample(hd(2 * c_conv), hd(2 * c_conv + c_attn), hd(2 * c_conv + 2 * c_attn),
                                   cache_k[i], cache_v[i], page_table, rel_bias)

            xs, ks, vs, cs = _conv_attn_layer(xs, state_conv[i], attn_s, *ab)
            hd4 = lambda t: t.reshape(t.shape[0], t.shape[1], n_heads_attn, HD_ATTN)
            outs["kp"].append(hd4(kp)); outs["vp"].append(hd4(vp))
            outs["ks"].append(hd4(ks)); outs["vs"].append(hd4(vs))
            outs["cp"].append(cp); outs["cs"].append(cs)
        else:
            p = dict(mix=rw_mix[i], wr=bf(rw_wr[i]), wk=bf(rw_wk[i]), wv=bf(rw_wv[i]), wo=bf(rw_wo[i]),
                     w0=rw_w0[i], w1=bf(rw_w1[i]), w2=bf(rw_w2[i]), a0=rw_a0[i], a1=bf(rw_a1[i]), a2=bf(rw_a2[i]),
                     g1=bf(rw_g1[i]), g2=bf(rw_g2[i]), kk=rw_kk[i], ka=rw_ka[i], rk=rw_rk[i].reshape(-1),
                     lnx_g=rw_lnx_g[i], lnx_b=rw_lnx_b[i])
            xp, wp, hp = _rwkv_layer(xp, jnp.zeros((B, D), F32),
                                     jnp.zeros((B, n_heads_rwkv, HS_RWKV, HS_RWKV), F32), p, g0, b0, alpha,
                                     n_heads_rwkv)
            xs, ws, hs = _rwkv_layer(xs, state_shift[i], state_wkv[i], p, g0, b0, alpha, n_heads_rwkv)
            outs["wp"].append(wp); outs["ws"].append(ws); outs["hp"].append(hp); outs["hs"].append(hs)
        w1b, w2b = bf(mlp_w1[layer]), bf(mlp_w2[layer])
        xp = mlp_ln(xp.reshape(B * L, D), w1b, w2b, g1, b1, alpha).reshape(B, L, D)
        xs = mlp_ln(xs.reshape(DB * DL, D), w1b, w2b, g1, b1, alpha).reshape(DB, DL, D)
    st = lambda n: jnp.stack(outs[n])
    return (xp, xs, st("kp"), st("vp"), st("ks"), st("vs"), st("cp"), st("cs"),
            st("wp"), st("ws"), st("hp"), st("hs"))
```

```python
import functools
import math

import jax
import jax.numpy as jnp
import numpy as np
from jax import lax
from jax.experimental import pallas as pl
from jax.experimental.pallas import tpu as pltpu

F32 = jnp.float32
BF16 = jnp.bfloat16
HIGHEST = lax.Precision.HIGHEST

LANES = 128
SUBLANES = 8
VMEM_LIMIT = 56 * 1024 * 1024

CONV_W = 31
HD_ATTN = 128
MOBA_BLOCK = 256
MOBA_TOPK = 3
NUM_BUCKETS = 32
MAX_DISTANCE = 128
HS_RWKV = 64
GN_EPS = 64e-5
LN_EPS = 1e-5
NEG = -1e30


def _params(sem):
    return pltpu.CompilerParams(dimension_semantics=sem, vmem_limit_bytes=VMEM_LIMIT)


def _pick(n, pref):
    if n <= pref:
        return n
    t = pref
    while n % t:
        t //= 2
    return t


def _softplus(y):
    return jnp.maximum(y, 0.0) + jnp.log(1.0 + jnp.exp(-jnp.abs(y)))


def _sigmoid(y):
    return 1.0 / (1.0 + jnp.exp(-y))


def _epilogue(acc, kind):
    if kind == "none":
        return acc
    if kind == "tanh":
        return jnp.tanh(acc)
    if kind == "sigmoid":
        return _sigmoid(acc)
    if kind == "decay":
        return jnp.exp(-jnp.exp(-_softplus(-acc) - 0.5))
    raise ValueError(kind)


def _mm_kernel(*refs, mix, bias, epilogue):
    it = iter(refs)
    x_ref = next(it)
    xp_ref = next(it) if mix else None
    m_ref = next(it) if mix else None
    w_ref = next(it)
    b_ref = next(it) if bias else None
    o_ref = next(it)
    xs_ref = next(it)

    @pl.when(pl.program_id(1) == 0)
    def _():
        x = x_ref[...]
        if mix:
            x = x + (xp_ref[...] - x) * m_ref[...]
        xs_ref[...] = x.astype(BF16)

    acc = jnp.dot(xs_ref[...], w_ref[...], preferred_element_type=F32)
    if bias:
        acc = acc + b_ref[...]
    o_ref[...] = _epilogue(acc, epilogue)


def matmul(x, w, *, xprev=None, mixrow=None, bias=None, epilogue="none", tm=512, tn=512):
    M, K = x.shape
    N = w.shape[1]
    tm = _pick(M, tm)
    tn = _pick(N, tn)
    mix = xprev is not None
    ins = [x]
    specs = [pl.BlockSpec((tm, K), lambda i, j: (i, 0))]
    if mix:
        ins += [xprev, mixrow]
        specs += [pl.BlockSpec((tm, K), lambda i, j: (i, 0)), pl.BlockSpec((1, K), lambda i, j: (0, 0))]
    ins.append(w)
    specs.append(pl.BlockSpec((K, tn), lambda i, j: (0, j)))
    if bias is not None:
        ins.append(bias)
        specs.append(pl.BlockSpec((1, tn), lambda i, j: (0, j)))
    return pl.pallas_call(
        functools.partial(_mm_kernel, mix=mix, bias=bias is not None, epilogue=epilogue),
        grid=(M // tm, N // tn),
        in_specs=specs,
        out_specs=pl.BlockSpec((tm, tn), lambda i, j: (i, j)),
        out_shape=jax.ShapeDtypeStruct((M, N), F32),
        scratch_shapes=[pltpu.VMEM((tm, K), BF16)],
        compiler_params=_params(("parallel", "arbitrary")),
    )(*ins)


def _layer_norm(z, g, b):
    mu = jnp.mean(z, axis=-1, keepdims=True)
    zc = z - mu
    var = jnp.mean(zc * zc, axis=-1, keepdims=True)
    return zc * lax.rsqrt(var + LN_EPS) * g + b


def _rowmm_ln_kernel(x_ref, w_ref, r_ref, g_ref, b_ref, o_ref, acc_ref, *, alpha):
    k = pl.program_id(1)

    @pl.when(k == 0)
    def _():
        acc_ref[...] = jnp.zeros_like(acc_ref)

    acc_ref[...] += jnp.dot(x_ref[...].astype(BF16), w_ref[...], preferred_element_type=F32)

    @pl.when(k == pl.num_programs(1) - 1)
    def _():
        o_ref[...] = _layer_norm(alpha * r_ref[...] + acc_ref[...], g_ref[...], b_ref[...])


def rowmm_ln(x, w, resid, g, b, alpha, *, tm=512, tk=512):
    M, K = x.shape
    N = w.shape[1]
    tm = _pick(M, tm)
    tk = _pick(K, tk)
    return pl.pallas_call(
        functools.partial(_rowmm_ln_kernel, alpha=alpha),
        grid=(M // tm, K // tk),
        in_specs=[
            pl.BlockSpec((tm, tk), lambda i, k: (i, k)),
            pl.BlockSpec((tk, N), lambda i, k: (k, 0)),
            pl.BlockSpec((tm, N), lambda i, k: (i, 0)),
            pl.BlockSpec((1, N), lambda i, k: (0, 0)),
            pl.BlockSpec((1, N), lambda i, k: (0, 0)),
        ],
        out_specs=pl.BlockSpec((tm, N), lambda i, k: (i, 0)),
        out_shape=jax.ShapeDtypeStruct((M, N), F32),
        scratch_shapes=[pltpu.VMEM((tm, N), F32)],
        compiler_params=_params(("parallel", "arbitrary")),
    )(x, w, resid, g, b)


def _mlp_ln_kernel(x_ref, w1_ref, w2_ref, g_ref, b_ref, o_ref, xs_ref, acc_ref, *, alpha):
    f = pl.program_id(1)

    @pl.when(f == 0)
    def _():
        xs_ref[...] = x_ref[...].astype(BF16)
        acc_ref[...] = jnp.zeros_like(acc_ref)

    h = jnp.dot(xs_ref[...], w1_ref[...], preferred_element_type=F32)
    h = jnp.maximum(h, 0.0)
    h = (h * h).astype(BF16)
    acc_ref[...] += jnp.dot(h, w2_ref[...], preferred_element_type=F32)

    @pl.when(f == pl.num_programs(1) - 1)
    def _():
        o_ref[...] = _layer_norm(alpha * x_ref[...] + acc_ref[...], g_ref[...], b_ref[...])


def mlp_ln(x, w1, w2, g, b, alpha, *, tm=1024, tf=512):
    M, D = x.shape
    FF = w1.shape[1]
    tm = _pick(M, tm)
    tf = _pick(FF, tf)
    once = pl.Buffered(1)
    return pl.pallas_call(
        functools.partial(_mlp_ln_kernel, alpha=alpha),
        grid=(M // tm, FF // tf),
        in_specs=[
            pl.BlockSpec((tm, D), lambda i, f: (i, 0), pipeline_mode=once),
            pl.BlockSpec((D, tf), lambda i, f: (0, f)),
            pl.BlockSpec((tf, D), lambda i, f: (f, 0)),
            pl.BlockSpec((1, D), lambda i, f: (0, 0)),
            pl.BlockSpec((1, D), lambda i, f: (0, 0)),
        ],
        out_specs=pl.BlockSpec((tm, D), lambda i, f: (i, 0)),
        out_shape=jax.ShapeDtypeStruct((M, D), F32),
        scratch_shapes=[pltpu.VMEM((tm, D), BF16), pltpu.VMEM((tm, D), F32)],
        compiler_params=_params(("parallel", "arbitrary")),
    )(x, w1, w2, g, b)


HALO = 32


def _conv_kernel(hv_ref, hg_ref, prev_ref, w_ref, cb_ref, g_ref, b_ref, c_ref, st_ref, ue_ref, *, tl):
    l = pl.program_id(1)
    npv = CONV_W - 1
    off = HALO - npv

    @pl.when(l == 0)
    def _():
        ue_ref[pl.ds(off, npv), :] = prev_ref[...]

    u = hv_ref[...] * _sigmoid(hg_ref[...])
    ue_ref[pl.ds(HALO, tl), :] = u
    acc = jnp.zeros_like(u) + cb_ref[...]
    for j in range(CONV_W):
        acc = acc + ue_ref[pl.ds(off + j, tl), :] * w_ref[pl.ds(j, 1), :]
    y = _layer_norm(acc, g_ref[...], b_ref[...])
    c_ref[...] = y * _sigmoid(y)
    tail = ue_ref[pl.ds(off + tl, npv), :]
    ue_ref[pl.ds(off, npv), :] = tail

    @pl.when(l == pl.num_programs(1) - 1)
    def _():
        st_ref[...] = tail


def conv_module(h3, conv_prev, conv_w, conv_b, ln_g, ln_b, c_conv, *, tl=256):
    B, L, _ = h3.shape
    tl = _pick(L, tl)
    npv = CONV_W - 1
    row = lambda a: a.reshape(1, c_conv)
    return pl.pallas_call(
        functools.partial(_conv_kernel, tl=tl),
        grid=(B, L // tl),
        in_specs=[
            pl.BlockSpec((None, tl, c_conv), lambda b, l: (b, l, 0)),
            pl.BlockSpec((None, tl, c_conv), lambda b, l: (b, l, 1)),
            pl.BlockSpec((None, npv, c_conv), lambda b, l: (b, 0, 0)),
            pl.BlockSpec((CONV_W, c_conv), lambda b, l: (0, 0)),
            pl.BlockSpec((1, c_conv), lambda b, l: (0, 0)),
            pl.BlockSpec((1, c_conv), lambda b, l: (0, 0)),
            pl.BlockSpec((1, c_conv), lambda b, l: (0, 0)),
        ],
        out_specs=[
            pl.BlockSpec((None, tl, c_conv), lambda b, l: (b, l, 0)),
            pl.BlockSpec((None, npv, c_conv), lambda b, l: (b, 0, 0)),
        ],
        out_shape=[
            jax.ShapeDtypeStruct((B, L, c_conv), F32),
            jax.ShapeDtypeStruct((B, npv, c_conv), F32),
        ],
        scratch_shapes=[pltpu.VMEM((HALO + tl, c_conv), F32)],
        compiler_params=_params(("arbitrary", "arbitrary")),
    )(h3, h3, conv_prev, conv_w, row(conv_b), row(ln_g), row(ln_b))


def _t5_bucket(dist):
    n = jnp.maximum(dist, 0)
    max_exact = NUM_BUCKETS // 2
    nf = jnp.maximum(n, 1).astype(F32)
    large = max_exact + (jnp.log(nf / max_exact) / math.log(MAX_DISTANCE / max_exact)
                         * (NUM_BUCKETS - max_exact)).astype(jnp.int32)
    large = jnp.minimum(large, NUM_BUCKETS - 1)
    return jnp.where(n < max_exact, n, large)


def _bias_of_dist(rel_bias, dist):
    onehot = (_t5_bucket(dist)[..., None] == jnp.arange(NUM_BUCKETS)).astype(F32)
    b = jnp.moveaxis(jnp.dot(onehot, rel_bias, precision=HIGHEST), -1, 0)
    return jnp.where(dist[None] >= 0, b, NEG)


def _kmean_kernel(k_ref, o_ref, *, nb):
    k = k_ref[...]
    km = jnp.mean(k.reshape(nb, MOBA_BLOCK, HD_ATTN), axis=1)
    o_ref[...] = jnp.zeros_like(o_ref)
    o_ref[pl.ds(0, nb), :] = km


def _select_topk(gate, valid, axis):
    idx = lax.broadcasted_iota(jnp.int32, gate.shape, axis).astype(F32)
    big = 1e9
    g = jnp.where(valid, gate, -jnp.inf)
    sel = jnp.zeros(gate.shape, jnp.bool_)
    for _ in range(MOBA_TOPK):
        mx = jnp.max(g, axis=axis, keepdims=True)
        is_max = (g == mx) & (mx > -jnp.inf)
        first = jnp.min(jnp.where(is_max, idx, big), axis=axis, keepdims=True)
        pick = idx == first
        sel = sel | pick
        g = jnp.where(pick, -jnp.inf, g)
    return jnp.where(sel, 0.0, NEG).astype(F32)


FAR_GROUP = 4


def _flash_kernel(q_ref, k_ref, v_ref, km_ref, bd_ref, bp_ref, bf_ref, o_ref,
                  kb_ref, vt_ref, qs_ref, sel_ref, m_ref, l_ref, acc_ref, *, nb):
    i = pl.program_id(1)
    T = MOBA_BLOCK

    @pl.when(i == 0)
    def _():
        def convert(n, carry):
            r0 = pl.multiple_of(n * T, T)
            kb_ref[n] = k_ref[pl.ds(r0, T), :].astype(BF16)
            vt_ref[n] = v_ref[pl.ds(r0, T), :].T.astype(BF16)
            return carry

        lax.fori_loop(0, nb, convert, 0)

    q = q_ref[...]
    qs_ref[...] = (q * (HD_ATTN ** -0.5)).astype(BF16)
    gate_t = lax.dot_general(km_ref[...], q, (((1,), (1,)), ((), ())),
                             precision=HIGHEST, preferred_element_type=F32)
    blk = lax.broadcasted_iota(jnp.int32, gate_t.shape, 0)
    sel_ref[...] = _select_topk(gate_t, blk < i, 0)
    m_ref[...] = jnp.full_like(m_ref, NEG)
    l_ref[...] = jnp.zeros_like(l_ref)
    acc_ref[...] = jnp.zeros_like(acc_ref)

    def update(blocks):
        qs = qs_ref[...]
        ss = [lax.dot_general(kb_ref[j], qs, (((1,), (1,)), ((), ())), preferred_element_type=F32) + extra
              for j, extra in blocks]
        m_old = m_ref[...]
        m_new = m_old
        for s in ss:
            m_new = jnp.maximum(m_new, jnp.max(s, axis=0, keepdims=True))
        alpha = jnp.exp(m_old - m_new)
        l_new = alpha * l_ref[...]
        acc = alpha * acc_ref[...]
        for (j, _), s in zip(blocks, ss):
            p = jnp.exp(s - m_new)
            l_new = l_new + jnp.sum(p, axis=0, keepdims=True)
            acc = acc + jnp.dot(vt_ref[j], p.astype(BF16), preferred_element_type=F32)
        l_ref[...] = l_new
        acc_ref[...] = acc
        m_ref[...] = m_new

    def far_term(j):
        return bf_ref[...] + sel_ref[pl.ds(j, 1), :]

    n_far = jnp.maximum(i - 1, 0)
    n_groups = n_far // FAR_GROUP

    def far_group(g, carry):
        j0 = g * FAR_GROUP
        update([(j0 + b, far_term(j0 + b)) for b in range(FAR_GROUP)])
        return carry

    def far_single(j, carry):
        update([(j, far_term(j))])
        return carry

    lax.fori_loop(0, n_groups, far_group, 0)
    lax.fori_loop(n_groups * FAR_GROUP, n_far, far_single, 0)

    @pl.when(i == 0)
    def _():
        update([(i, bd_ref[...])])

    @pl.when(i >= 1)
    def _():
        update([(i - 1, bp_ref[...] + sel_ref[pl.ds(i - 1, 1), :]), (i, bd_ref[...])])

    o_ref[...] = (acc_ref[...] / l_ref[...]).T


def moba_prompt(h2, rel_bias, n_heads, q_col, k_col, v_col):
    L = h2.shape[0]
    nb = L // MOBA_BLOCK
    assert L % MOBA_BLOCK == 0 and nb <= LANES
    T = MOBA_BLOCK
    kmean = pl.pallas_call(
        functools.partial(_kmean_kernel, nb=nb),
        grid=(n_heads,),
        in_specs=[pl.BlockSpec((L, HD_ATTN), lambda h: (0, k_col + h))],
        out_specs=pl.BlockSpec((None, LANES, HD_ATTN), lambda h: (h, 0, 0)),
        out_shape=jax.ShapeDtypeStruct((n_heads, LANES, HD_ATTN), F32),
        compiler_params=_params(("parallel",)),
    )(h2)

    ki = jnp.arange(T)[:, None]
    qi = jnp.arange(T)[None, :]
    bias_diag = _bias_of_dist(rel_bias, qi - ki)
    bias_prev = _bias_of_dist(rel_bias, qi - ki + T)
    bias_far = jnp.broadcast_to(rel_bias[NUM_BUCKETS - 1][:, None, None], (n_heads, 1, T))

    return pl.pallas_call(
        functools.partial(_flash_kernel, nb=nb),
        grid=(n_heads, nb),
        in_specs=[
            pl.BlockSpec((T, HD_ATTN), lambda h, i: (i, q_col + h)),
            pl.BlockSpec((L, HD_ATTN), lambda h, i: (0, k_col + h)),
            pl.BlockSpec((L, HD_ATTN), lambda h, i: (0, v_col + h)),
            pl.BlockSpec((None, LANES, HD_ATTN), lambda h, i: (h, 0, 0)),
            pl.BlockSpec((None, T, T), lambda h, i: (h, 0, 0)),
            pl.BlockSpec((None, T, T), lambda h, i: (h, 0, 0)),
            pl.BlockSpec((None, 1, T), lambda h, i: (h, 0, 0)),
        ],
        out_specs=pl.BlockSpec((T, HD_ATTN), lambda h, i: (i, h)),
        out_shape=jax.ShapeDtypeStruct((L, n_heads * HD_ATTN), F32),
        scratch_shapes=[
            pltpu.VMEM((nb, T, HD_ATTN), BF16),
            pltpu.VMEM((nb, HD_ATTN, T), BF16),
            pltpu.VMEM((T, HD_ATTN), BF16),
            pltpu.VMEM((LANES, T), F32),
            pltpu.VMEM((1, T), F32),
            pltpu.VMEM((1, T), F32),
            pltpu.VMEM((HD_ATTN, T), F32),
        ],
        compiler_params=_params(("parallel", "arbitrary")),
    )(h2, h2, h2, kmean, bias_diag, bias_prev, bias_far)


QPAD = 8
SAMPLE_BLOCKS_PER_STEP = 2


def _sample_p1_kernel(pt_ref, *refs, bps):
    n = pl.program_id(1)
    page_refs = refs[:2 * bps]
    w_ref, lg_ref, gate_ref = refs[2 * bps:]
    w = w_ref[...].astype(BF16)
    P = page_refs[0].shape[0]
    for blk in range(bps):
        gsum = jnp.zeros((1, w.shape[1]), F32)
        for half in range(2):
            kb = page_refs[2 * blk + half][...].astype(BF16)
            s = jnp.dot(kb, w, preferred_element_type=F32)
            gsum = gsum + jnp.sum(s, axis=0, keepdims=True)
            lg_ref[blk, pl.ds(half * P, P), :] = s * (HD_ATTN ** -0.5)
        gate_ref[pl.ds(n * bps + blk, 1), :] = gsum * (1.0 / MOBA_BLOCK)


def _sample_p2_kernel(lg_ref, gate_ref, kn_ref, w_ref, bl_ref, bf_ref, bo_ref, p_ref, pn_ref, *, nbp):
    gate = gate_ref[...]
    selm = _select_topk(gate, jnp.ones(gate.shape, jnp.bool_), 0)
    s = lg_ref[...] + selm[:, None, :]
    blk = lax.broadcasted_iota(jnp.int32, (nbp, 1, 1), 0)
    s = s + jnp.where(blk == nbp - 1, bl_ref[...][None], bf_ref[...][None])
    w = w_ref[...]
    sn = jnp.dot(kn_ref[...].astype(BF16), w.astype(BF16), preferred_element_type=F32) * (HD_ATTN ** -0.5)
    sn = sn + bo_ref[...]
    m = jnp.maximum(jnp.max(jnp.max(s, axis=1), axis=0, keepdims=True), jnp.max(sn, axis=0, keepdims=True))
    p = jnp.exp(s - m[None])
    pn = jnp.exp(sn - m)
    den = jnp.sum(jnp.sum(p, axis=1), axis=0, keepdims=True) + jnp.sum(pn, axis=0, keepdims=True)
    inv = 1.0 / den
    p_ref[...] = p * inv[None]
    pn_ref[...] = pn * inv


def _sample_p3_kernel(pt_ref, *refs, n_heads, bps):
    n = pl.program_id(1)
    page_refs = refs[:2 * bps]
    p_ref, pn_ref, vn_ref, o_ref, acc_ref = refs[2 * bps:]
    P = page_refs[0].shape[0]

    @pl.when(n == 0)
    def _():
        acc_ref[...] = lax.dot_general(pn_ref[...].astype(BF16), vn_ref[...].astype(BF16),
                                       (((0,), (0,)), ((), ())), preferred_element_type=F32)

    acc = acc_ref[...]
    for blk in range(bps):
        for half in range(2):
            p = p_ref[blk, pl.ds(half * P, P), :].astype(BF16)
            acc = acc + lax.dot_general(p, page_refs[2 * blk + half][...].astype(BF16), (((0,), (0,)), ((), ())),
                                        preferred_element_type=F32)
    acc_ref[...] = acc

    @pl.when(n == pl.num_programs(1) - 1)
    def _():
        for h in range(n_heads):
            o_ref[:, pl.ds(h * HD_ATTN, HD_ATTN)] = acc_ref[pl.ds(h * QPAD, QPAD), pl.ds(h * HD_ATTN, HD_ATTN)]


def moba_sample(q, k_new, v_new, cache_k, cache_v, layer, page_table, rel_bias):
    B, L, H, HD = q.shape
    page = cache_k.shape[2]
    n_pages = page_table.shape[1]
    past = n_pages * page
    assert MOBA_BLOCK == 2 * page and past % MOBA_BLOCK == 0 and L <= QPAD
    nbp = past // MOBA_BLOCK
    C = H * QPAD
    CH = H * HD
    ck = cache_k.reshape(cache_k.shape[0], cache_k.shape[1], page, CH)
    cv = cache_v.reshape(cache_v.shape[0], cache_v.shape[1], page, CH)

    qp = jnp.pad(q, ((0, 0), (0, QPAD - L), (0, 0), (0, 0)))
    eye = jnp.eye(H, dtype=F32)
    wbd = jnp.einsum("bqhd,hg->bhdgq", qp, eye).reshape(B, CH, C)

    qpos = past + jnp.arange(QPAD)
    kl = (nbp - 1) * MOBA_BLOCK + jnp.arange(MOBA_BLOCK)
    bias_last = _bias_of_dist(rel_bias, qpos[None, :] - kl[:, None])
    bias_last = jnp.moveaxis(bias_last, 0, 1).reshape(MOBA_BLOCK, C)
    bias_far = jnp.repeat(rel_bias[NUM_BUCKETS - 1], QPAD).reshape(1, C)
    kown = past + jnp.arange(QPAD)
    dist_own = qpos[None, :] - kown[:, None]
    ok = (jnp.arange(QPAD)[:, None] < L) & (jnp.arange(QPAD)[None, :] < L)
    bias_own = _bias_of_dist(rel_bias, jnp.where(ok, dist_own, -1))
    bias_own = jnp.moveaxis(bias_own, 0, 1).reshape(QPAD, C)

    knp = jnp.pad(k_new.reshape(B, L, CH), ((0, 0), (0, QPAD - L), (0, 0)))
    vnp = jnp.pad(v_new.reshape(B, L, CH), ((0, 0), (0, QPAD - L), (0, 0)))

    bps = _pick(nbp, SAMPLE_BLOCKS_PER_STEP)
    page_specs = [pl.BlockSpec((None, None, page, CH),
                               functools.partial(lambda b, n, pt, j: (layer, pt[b, 2 * bps * n + j], 0, 0), j=j))
                  for j in range(2 * bps)]
    logits, gate = pl.pallas_call(
        functools.partial(_sample_p1_kernel, bps=bps),
        grid_spec=pltpu.PrefetchScalarGridSpec(
            num_scalar_prefetch=1,
            grid=(B, nbp // bps),
            in_specs=page_specs + [pl.BlockSpec((None, CH, C), lambda b, n, pt: (b, 0, 0))],
            out_specs=[pl.BlockSpec((None, bps, MOBA_BLOCK, C), lambda b, n, pt: (b, n, 0, 0)),
                       pl.BlockSpec((None, nbp, C), lambda b, n, pt: (b, 0, 0))],
        ),
        out_shape=[jax.ShapeDtypeStruct((B, nbp, MOBA_BLOCK, C), F32),
                   jax.ShapeDtypeStruct((B, nbp, C), F32)],
        compiler_params=_params(("parallel", "arbitrary")),
    )(page_table, *([ck] * (2 * bps)), wbd)

    p_past, p_new = pl.pallas_call(
        functools.partial(_sample_p2_kernel, nbp=nbp),
        grid=(B,),
        in_specs=[
            pl.BlockSpec((None, nbp, MOBA_BLOCK, C), lambda b: (b, 0, 0, 0)),
            pl.BlockSpec((None, nbp, C), lambda b: (b, 0, 0)),
            pl.BlockSpec((None, QPAD, CH), lambda b: (b, 0, 0)),
            pl.BlockSpec((None, CH, C), lambda b: (b, 0, 0)),
            pl.BlockSpec((MOBA_BLOCK, C), lambda b: (0, 0)),
            pl.BlockSpec((1, C), lambda b: (0, 0)),
            pl.BlockSpec((QPAD, C), lambda b: (0, 0)),
        ],
        out_specs=[pl.BlockSpec((None, nbp, MOBA_BLOCK, C), lambda b: (b, 0, 0, 0)),
                   pl.BlockSpec((None, QPAD, C), lambda b: (b, 0, 0))],
        out_shape=[jax.ShapeDtypeStruct((B, nbp, MOBA_BLOCK, C), F32),
                   jax.ShapeDtypeStruct((B, QPAD, C), F32)],
        compiler_params=_params(("parallel",)),
    )(logits, gate, knp, wbd, bias_last, bias_far, bias_own)

    o = pl.pallas_call(
        functools.partial(_sample_p3_kernel, n_heads=H, bps=bps),
        grid_spec=pltpu.PrefetchScalarGridSpec(
            num_scalar_prefetch=1,
            grid=(B, nbp // bps),
            in_specs=page_specs + [
                pl.BlockSpec((None, bps, MOBA_BLOCK, C), lambda b, n, pt: (b, n, 0, 0)),
                pl.BlockSpec((None, QPAD, C), lambda b, n, pt: (b, 0, 0)),
                pl.BlockSpec((None, QPAD, CH), lambda b, n, pt: (b, 0, 0))],
            out_specs=pl.BlockSpec((None, QPAD, CH), lambda b, n, pt: (b, 0, 0)),
            scratch_shapes=[pltpu.VMEM((C, CH), F32)],
        ),
        out_shape=jax.ShapeDtypeStruct((B, QPAD, CH), F32),
        compiler_params=_params(("parallel", "arbitrary")),
    )(page_table, *([cv] * (2 * bps)), p_past, p_new, vnp)
    return o[:, :L]


def _seg_ones():
    r = lax.broadcasted_iota(jnp.int32, (LANES, LANES), 0) // HS_RWKV
    c = lax.broadcasted_iota(jnp.int32, (LANES, LANES), 1) // HS_RWKV
    return (r == c).astype(F32)


def _head_sum(x):
    ones = _seg_ones()
    parts = [jnp.dot(x[:, c:c + LANES], ones, precision=HIGHEST, preferred_element_type=F32)
             for c in range(0, x.shape[1], LANES)]
    return jnp.concatenate(parts, axis=1)


def _rwkv_prep_kernel(k_ref, a_ref, kk_ref, ka_ref, k2_ref, na_ref, nb_ref):
    k = k_ref[...]
    a = a_ref[...]
    kk = k * kk_ref[...]
    nrm = jnp.maximum(jnp.sqrt(_head_sum(kk * kk)), 1e-12)
    kk = kk / nrm
    k2_ref[...] = k * (1.0 + (a - 1.0) * ka_ref[...])
    na_ref[...] = -kk
    nb_ref[...] = kk * a


def rwkv_prep(k, a, k_k, k_a, *, tm=256):
    M, D = k.shape
    tm = _pick(M, tm)
    blk = pl.BlockSpec((tm, D), lambda i: (i, 0))
    row = pl.BlockSpec((1, D), lambda i: (0, 0))
    return pl.pallas_call(
        _rwkv_prep_kernel,
        grid=(M // tm,),
        in_specs=[blk, blk, row, row],
        out_specs=[blk, blk, blk],
        out_shape=[jax.ShapeDtypeStruct((M, D), F32)] * 3,
        compiler_params=_params(("parallel",)),
    )(k, a, k_k.reshape(1, D), k_a.reshape(1, D))


def _rwkv_post_kernel(y_ref, r_ref, k2_ref, v_ref, g_ref, rk_ref, lg_ref, lb_ref, o_ref):
    y = y_ref[...]
    inv = 1.0 / HS_RWKV
    mu = _head_sum(y) * inv
    yc = y - mu
    var = _head_sum(yc * yc) * inv
    yn = yc * lax.rsqrt(var + GN_EPS) * lg_ref[...] + lb_ref[...]
    bonus = _head_sum(r_ref[...] * k2_ref[...] * rk_ref[...]) * v_ref[...]
    o_ref[...] = (yn + bonus) * g_ref[...]


def rwkv_post(y, r, k2, v, g, r_k, lnx_g, lnx_b, *, tm=256):
    M, D = y.shape
    tm = _pick(M, tm)
    blk = pl.BlockSpec((tm, D), lambda i: (i, 0))
    row = pl.BlockSpec((1, D), lambda i: (0, 0))
    return pl.pallas_call(
        _rwkv_post_kernel,
        grid=(M // tm,),
        in_specs=[blk] * 5 + [row] * 3,
        out_specs=blk,
        out_shape=jax.ShapeDtypeStruct((M, D), F32),
        compiler_params=_params(("parallel",)),
    )(y, r, k2, v, g, r_k.reshape(1, D), lnx_g.reshape(1, D), lnx_b.reshape(1, D))


def _split3(x):
    hi = x.astype(BF16)
    r1 = x - hi.astype(F32)
    mid = r1.astype(BF16)
    lo = (r1 - mid.astype(F32)).astype(BF16)
    return hi, mid, lo


def _scan_kernel(r_ref, d_ref, k_ref, a_ref, b_ref, v_ref, s0_ref, y_ref, so_ref, cola_ref, colb_ref, s_ref,
                 *, ng, tg, nh, nvh):
    c = pl.program_id(1)

    @pl.when(c == 0)
    def _():
        s_ref[...] = s0_ref[...]

    rr = lax.broadcasted_iota(jnp.int32, (LANES, tg * LANES), 0)
    cc = lax.broadcasted_iota(jnp.int32, (LANES, tg * LANES), 1)
    spread = ((rr // nh == cc // LANES) & (rr % nh == cc % nh)).astype(BF16)

    def build(gi, col_ref):
        for o, ref in enumerate((r_ref, d_ref, k_ref, a_ref, b_ref)):
            col_ref[o] = sum(jnp.dot(part, spread, preferred_element_type=F32) for part in _split3(ref[gi]))

    def consume(gi, col_ref):
        for tt in range(tg):
            lanes = pl.ds(tt * LANES, LANES)
            r_c = col_ref[0, :, lanes]
            d_c = col_ref[1, :, lanes]
            k_c = col_ref[2, :, lanes]
            a_c = col_ref[3, :, lanes]
            b_c = col_ref[4, :, lanes]
            t = gi * tg + tt
            vt = v_ref[t]
            rows = []
            for vh in range(nvh):
                S = s_ref[vh]
                sa = jnp.sum(S * a_c, axis=0, keepdims=True)
                S = S * d_c + sa * b_c + vt[vh:vh + 1, :] * k_c
                s_ref[vh] = S
                rows.append(jnp.sum(S * r_c, axis=0, keepdims=True))
            y_ref[t] = jnp.concatenate(rows, axis=0)

    build(0, cola_ref)
    if ng % 2:
        assert ng == 1
        consume(0, cola_ref)
    else:
        def pair(gp, carry):
            g0 = 2 * gp
            build(g0 + 1, colb_ref)
            consume(g0, cola_ref)
            build(jnp.minimum(g0 + 2, ng - 1), cola_ref)
            consume(g0 + 1, colb_ref)
            return carry

        lax.fori_loop(0, ng // 2, pair, 0)

    @pl.when(c == pl.num_programs(1) - 1)
    def _():
        so_ref[...] = s_ref[...]


def rwkv_scan(r, d, k, a, b, v, state, n_heads, *, tc=64):
    B, L, D = r.shape
    H = n_heads
    vlo = LANES // H
    nvh = HS_RWKV // vlo
    tg = LANES // H
    assert H * vlo == LANES and nvh * vlo == HS_RWKV and L % tg == 0
    tc = _pick(L, tc)
    ng = tc // tg
    kmaj = lambda x: (x.reshape(B, L // tg, tg, H, HS_RWKV).transpose(0, 1, 4, 2, 3)
                      .reshape(B, L // tg, HS_RWKV, LANES))
    vperm = lambda x: x.reshape(B, L, H, nvh, vlo).transpose(0, 1, 3, 4, 2).reshape(B, L, nvh, LANES)
    s0 = state.reshape(B, H, nvh, vlo, HS_RWKV).transpose(0, 2, 4, 3, 1).reshape(B, nvh, HS_RWKV, LANES)
    kspec = pl.BlockSpec((None, ng, HS_RWKV, LANES), lambda bb, c: (bb, c, 0, 0))
    vspec = pl.BlockSpec((None, tc, nvh, LANES), lambda bb, c: (bb, c, 0, 0))
    sspec = pl.BlockSpec((None, nvh, HS_RWKV, LANES), lambda bb, c: (bb, 0, 0, 0))
    y, s1 = pl.pallas_call(
        functools.partial(_scan_kernel, ng=ng, tg=tg, nh=H, nvh=nvh),
        grid=(B, L // tc),
        in_specs=[kspec] * 5 + [vspec, sspec],
        out_specs=[vspec, sspec],
        out_shape=[jax.ShapeDtypeStruct((B, L, nvh, LANES), F32),
                   jax.ShapeDtypeStruct((B, nvh, HS_RWKV, LANES), F32)],
        scratch_shapes=[pltpu.VMEM((5, HS_RWKV, tg * LANES), F32),
                        pltpu.VMEM((5, HS_RWKV, tg * LANES), F32),
                        pltpu.VMEM((nvh, HS_RWKV, LANES), F32)],
        compiler_params=_params(("parallel", "arbitrary")),
    )(kmaj(r), kmaj(d), kmaj(k), kmaj(a), kmaj(b), vperm(v), s0)
    y = y.reshape(B, L, nvh, vlo, H).transpose(0, 1, 4, 2, 3).reshape(B, L, D)
    s1 = s1.reshape(B, nvh, HS_RWKV, vlo, H).transpose(0, 4, 1, 3, 2).reshape(B, H, HS_RWKV, HS_RWKV)
    return y, s1


def _conv_attn_layer(x, conv_prev, attn_fn, w_in, conv_w, conv_b, cln_g, cln_b, w_out, ln_g, ln_b, alpha,
                     c_conv, c_attn):
    B, L, D = x.shape
    x2 = x.reshape(B * L, D)
    h = matmul(x2, w_in).reshape(B, L, -1)
    c, cstate = conv_module(h, conv_prev, conv_w, conv_b, cln_g, cln_b, c_conv)
    o = attn_fn(h)
    co = jnp.concatenate([c, o], axis=-1).reshape(B * L, c_conv + c_attn)
    xn = rowmm_ln(co, w_out, x2, ln_g, ln_b, alpha).reshape(B, L, D)
    k = h[..., 2 * c_conv + c_attn:2 * c_conv + 2 * c_attn]
    v = h[..., 2 * c_conv + 2 * c_attn:]
    return xn, k, v, cstate


def _rwkv_layer(x, shift_prev, wkv_prev, p, ln_g, ln_b, alpha, n_heads):
    B, L, D = x.shape
    x_prev = jnp.concatenate([shift_prev[:, None], x[:, :-1]], axis=1).reshape(B * L, D)
    x2 = x.reshape(B * L, D)
    mixrow = lambda i: p["mix"][i].reshape(1, D)
    proj = lambda i, w, **kw: matmul(x2, w, xprev=x_prev, mixrow=mixrow(i), **kw)
    r = proj(0, p["wr"])
    lw = proj(1, p["w1"], epilogue="tanh")
    d = matmul(lw, p["w2"], bias=p["w0"].reshape(1, D), epilogue="decay")
    k = proj(2, p["wk"])
    v = proj(3, p["wv"])
    la = proj(4, p["a1"])
    a = matmul(la, p["a2"], bias=p["a0"].reshape(1, D), epilogue="sigmoid")
    lg = proj(5, p["g1"], epilogue="sigmoid")
    g = matmul(lg, p["g2"])
    k2, na, nb = rwkv_prep(k, a, p["kk"], p["ka"])
    sh = lambda t: t.reshape(B, L, D)
    y, s1 = rwkv_scan(sh(r), sh(d), sh(k2), sh(na), sh(nb), sh(v), wkv_prev, n_heads)
    yg = rwkv_post(y.reshape(B * L, D), r, k2, v, g, p["rk"], p["lnx_g"], p["lnx_b"])
    xn = rowmm_ln(yg, p["wo"], x2, ln_g, ln_b, alpha).reshape(B, L, D)
    return xn, s1, x[:, -1]


def kernel(x_prompt, x_sample, cache_k, cache_v, page_table, state_conv, state_wkv, state_shift, rel_bias, ab_w_in, ab_conv_w, ab_conv_b, ab_ln_g, ab_ln_b, ab_w_out, rw_mix, rw_wr, rw_wk, rw_wv, rw_wo, rw_w0, rw_w1, rw_w2, rw_a0, rw_a1, rw_a2, rw_g1, rw_g2, rw_kk, rw_ka, rw_rk, rw_lnx_g, rw_lnx_b, ln_g, ln_b, mlp_w1, mlp_w2):
    B, L, D = x_prompt.shape
    DB, DL, _ = x_sample.shape
    depth = ln_g.shape[0]
    alpha = (2 * depth) ** 0.25
    c_conv = ab_conv_w.shape[2]
    n_heads_attn = cache_k.shape[3]
    c_attn = n_heads_attn * HD_ATTN
    n_heads_rwkv = D // HS_RWKV
    assert B == 1
    bf = lambda w: w.astype(BF16)
    row = lambda a: a.reshape(1, -1)
    q_col = 2 * c_conv // LANES
    k_col = q_col + c_attn // LANES
    v_col = k_col + c_attn // LANES

    xp, xs = x_prompt, x_sample
    outs = {n: [] for n in ("kp", "vp", "ks", "vs", "cp", "cs", "wp", "ws", "hp", "hs")}
    for layer in range(depth):
        i = layer // 2
        g0, b0, g1, b1 = row(ln_g[layer, 0]), row(ln_b[layer, 0]), row(ln_g[layer, 1]), row(ln_b[layer, 1])
        if layer % 2 == 0:
            ab = (bf(ab_w_in[i]), ab_conv_w[i], ab_conv_b[i], ab_ln_g[i], ab_ln_b[i], bf(ab_w_out[i]), g0, b0, alpha,
                  c_conv, c_attn)
            attn_p = lambda h: moba_prompt(h.reshape(L, -1), rel_bias, n_heads_attn, q_col, k_col, v_col)[None]
            xp, kp, vp, cp = _conv_attn_layer(xp, jnp.zeros((B, CONV_W - 1, c_conv), F32), attn_p, *ab)

            def attn_s(h):
                hd = lambda lo: h[..., lo:lo + c_attn].reshape(DB, DL, n_heads_attn, HD_ATTN)
                return moba_sample(hd(2 * c_conv), hd(2 * c_conv + c_attn), hd(2 * c_conv + 2 * c_attn),
                                   cache_k, cache_v, i, page_table, rel_bias)

            xs, ks, vs, cs = _conv_attn_layer(xs, state_conv[i], attn_s, *ab)
            hd4 = lambda t: t.reshape(t.shape[0], t.shape[1], n_heads_attn, HD_ATTN)
            outs["kp"].append(hd4(kp)); outs["vp"].append(hd4(vp))
            outs["ks"].append(hd4(ks)); outs["vs"].append(hd4(vs))
            outs["cp"].append(cp); outs["cs"].append(cs)
        else:
            p = dict(mix=rw_mix[i], wr=bf(rw_wr[i]), wk=bf(rw_wk[i]), wv=bf(rw_wv[i]), wo=bf(rw_wo[i]),
                     w0=rw_w0[i], w1=bf(rw_w1[i]), w2=bf(rw_w2[i]), a0=rw_a0[i], a1=bf(rw_a1[i]), a2=bf(rw_a2[i]),
                     g1=bf(rw_g1[i]), g2=bf(rw_g2[i]), kk=rw_kk[i], ka=rw_ka[i], rk=rw_rk[i].reshape(-1),
                     lnx_g=rw_lnx_g[i], lnx_b=rw_lnx_b[i])
            xp, wp, hp = _rwkv_layer(xp, jnp.zeros((B, D), F32),
                                     jnp.zeros((B, n_heads_rwkv, HS_RWKV, HS_RWKV), F32), p, g0, b0, alpha,
                                     n_heads_rwkv)
            xs, ws, hs = _rwkv_layer(xs, state_shift[i], state_wkv[i], p, g0, b0, alpha, n_heads_rwkv)
            outs["wp"].append(wp); outs["ws"].append(ws); outs["hp"].append(hp); outs["hs"].append(hs)
        w1b, w2b = bf(mlp_w1[layer]), bf(mlp_w2[layer])
        xp = mlp_ln(xp.reshape(B * L, D), w1b, w2b, g1, b1, alpha).reshape(B, L, D)
        xs = mlp_ln(xs.reshape(DB * DL, D), w1b, w2b, g1, b1, alpha).reshape(DB, DL, D)
    st = lambda n: jnp.stack(outs[n])
    return (xp, xs, st("kp"), st("vp"), st("ks"), st("vs"), st("cp"), st("cs"),
            st("wp"), st("ws"), st("hp"), st("hs"))
```

```python
import functools
import math

import jax
import jax.numpy as jnp
import numpy as np
from jax import lax
from jax.experimental import pallas as pl
from jax.experimental.pallas import tpu as pltpu

F32 = jnp.float32
BF16 = jnp.bfloat16
HIGHEST = lax.Precision.HIGHEST

LANES = 128
SUBLANES = 8
VMEM_LIMIT = 56 * 1024 * 1024

CONV_W = 31
HD_ATTN = 128
MOBA_BLOCK = 256
MOBA_TOPK = 3
NUM_BUCKETS = 32
MAX_DISTANCE = 128
HS_RWKV = 64
GN_EPS = 64e-5
LN_EPS = 1e-5
NEG = -1e30


def _params(sem):
    return pltpu.CompilerParams(dimension_semantics=sem, vmem_limit_bytes=VMEM_LIMIT)


def _pick(n, pref):
    if n <= pref:
        return n
    t = pref
    while n % t:
        t //= 2
    return t


def _softplus(y):
    return jnp.maximum(y, 0.0) + jnp.log(1.0 + jnp.exp(-jnp.abs(y)))


def _sigmoid(y):
    return 1.0 / (1.0 + jnp.exp(-y))


def _epilogue(acc, kind):
    if kind == "none":
        return acc
    if kind == "tanh":
        return jnp.tanh(acc)
    if kind == "sigmoid":
        return _sigmoid(acc)
    if kind == "decay":
        return jnp.exp(-jnp.exp(-_softplus(-acc) - 0.5))
    raise ValueError(kind)


def _mm_kernel(*refs, mix, bias, epilogue):
    it = iter(refs)
    x_ref = next(it)
    xp_ref = next(it) if mix else None
    m_ref = next(it) if mix else None
    w_ref = next(it)
    b_ref = next(it) if bias else None
    o_ref = next(it)
    xs_ref = next(it)

    @pl.when(pl.program_id(1) == 0)
    def _():
        x = x_ref[...]
        if mix:
            x = x + (xp_ref[...] - x) * m_ref[...]
        xs_ref[...] = x.astype(BF16)

    acc = jnp.dot(xs_ref[...], w_ref[...], preferred_element_type=F32)
    if bias:
        acc = acc + b_ref[...]
    o_ref[...] = _epilogue(acc, epilogue)


def matmul(x, w, *, xprev=None, mixrow=None, bias=None, epilogue="none", tm=512, tn=512):
    M, K = x.shape
    N = w.shape[1]
    tm = _pick(M, tm)
    tn = _pick(N, tn)
    mix = xprev is not None
    ins = [x]
    specs = [pl.BlockSpec((tm, K), lambda i, j: (i, 0))]
    if mix:
        ins += [xprev, mixrow]
        specs += [pl.BlockSpec((tm, K), lambda i, j: (i, 0)), pl.BlockSpec((1, K), lambda i, j: (0, 0))]
    ins.append(w)
    specs.append(pl.BlockSpec((K, tn), lambda i, j: (0, j)))
    if bias is not None:
        ins.append(bias)
        specs.append(pl.BlockSpec((1, tn), lambda i, j: (0, j)))
    return pl.pallas_call(
        functools.partial(_mm_kernel, mix=mix, bias=bias is not None, epilogue=epilogue),
        grid=(M // tm, N // tn),
        in_specs=specs,
        out_specs=pl.BlockSpec((tm, tn), lambda i, j: (i, j)),
        out_shape=jax.ShapeDtypeStruct((M, N), F32),
        scratch_shapes=[pltpu.VMEM((tm, K), BF16)],
        compiler_params=_params(("parallel", "arbitrary")),
    )(*ins)


def _layer_norm(z, g, b):
    mu = jnp.mean(z, axis=-1, keepdims=True)
    zc = z - mu
    var = jnp.mean(zc * zc, axis=-1, keepdims=True)
    return zc * lax.rsqrt(var + LN_EPS) * g + b


def _rowmm_ln_kernel(x_ref, w_ref, r_ref, g_ref, b_ref, o_ref, acc_ref, *, alpha):
    k = pl.program_id(1)

    @pl.when(k == 0)
    def _():
        acc_ref[...] = jnp.zeros_like(acc_ref)

    acc_ref[...] += jnp.dot(x_ref[...].astype(BF16), w_ref[...], preferred_element_type=F32)

    @pl.when(k == pl.num_programs(1) - 1)
    def _():
        o_ref[...] = _layer_norm(alpha * r_ref[...] + acc_ref[...], g_ref[...], b_ref[...])


def rowmm_ln(x, w, resid, g, b, alpha, *, tm=512, tk=512):
    M, K = x.shape
    N = w.shape[1]
    tm = _pick(M, tm)
    tk = _pick(K, tk)
    return pl.pallas_call(
        functools.partial(_rowmm_ln_kernel, alpha=alpha),
        grid=(M // tm, K // tk),
        in_specs=[
            pl.BlockSpec((tm, tk), lambda i, k: (i, k)),
            pl.BlockSpec((tk, N), lambda i, k: (k, 0)),
            pl.BlockSpec((tm, N), lambda i, k: (i, 0)),
            pl.BlockSpec((1, N), lambda i, k: (0, 0)),
            pl.BlockSpec((1, N), lambda i, k: (0, 0)),
        ],
        out_specs=pl.BlockSpec((tm, N), lambda i, k: (i, 0)),
        out_shape=jax.ShapeDtypeStruct((M, N), F32),
        scratch_shapes=[pltpu.VMEM((tm, N), F32)],
        compiler_params=_params(("parallel", "arbitrary")),
    )(x, w, resid, g, b)


def _mlp_ln_kernel(x_ref, w1_ref, w2_ref, g_ref, b_ref, o_ref, xs_ref, acc_ref, *, alpha):
    f = pl.program_id(1)

    @pl.when(f == 0)
    def _():
        xs_ref[...] = x_ref[...].astype(BF16)
        acc_ref[...] = jnp.zeros_like(acc_ref)

    h = jnp.dot(xs_ref[...], w1_ref[...], preferred_element_type=F32)
    h = jnp.maximum(h, 0.0)
    h = (h * h).astype(BF16)
    acc_ref[...] += jnp.dot(h, w2_ref[...], preferred_element_type=F32)

    @pl.when(f == pl.num_programs(1) - 1)
    def _():
        o_ref[...] = _layer_norm(alpha * x_ref[...] + acc_ref[...], g_ref[...], b_ref[...])


def mlp_ln(x, w1, w2, g, b, alpha, *, tm=1024, tf=512):
    M, D = x.shape
    FF = w1.shape[1]
    tm = _pick(M, tm)
    tf = _pick(FF, tf)
    once = pl.Buffered(1)
    return pl.pallas_call(
        functools.partial(_mlp_ln_kernel, alpha=alpha),
        grid=(M // tm, FF // tf),
        in_specs=[
            pl.BlockSpec((tm, D), lambda i, f: (i, 0), pipeline_mode=once),
            pl.BlockSpec((D, tf), lambda i, f: (0, f)),
            pl.BlockSpec((tf, D), lambda i, f: (f, 0)),
            pl.BlockSpec((1, D), lambda i, f: (0, 0)),
            pl.BlockSpec((1, D), lambda i, f: (0, 0)),
        ],
        out_specs=pl.BlockSpec((tm, D), lambda i, f: (i, 0)),
        out_shape=jax.ShapeDtypeStruct((M, D), F32),
        scratch_shapes=[pltpu.VMEM((tm, D), BF16), pltpu.VMEM((tm, D), F32)],
        compiler_params=_params(("parallel", "arbitrary")),
    )(x, w1, w2, g, b)


HALO = 32


def _conv_kernel(hv_ref, hg_ref, prev_ref, w_ref, cb_ref, g_ref, b_ref, c_ref, st_ref, ue_ref, *, tl):
    l = pl.program_id(1)
    npv = CONV_W - 1
    off = HALO - npv

    @pl.when(l == 0)
    def _():
        ue_ref[pl.ds(off, npv), :] = prev_ref[...]

    u = hv_ref[...] * _sigmoid(hg_ref[...])
    ue_ref[pl.ds(HALO, tl), :] = u
    acc = jnp.zeros_like(u) + cb_ref[...]
    for j in range(CONV_W):
        acc = acc + ue_ref[pl.ds(off + j, tl), :] * w_ref[pl.ds(j, 1), :]
    y = _layer_norm(acc, g_ref[...], b_ref[...])
    c_ref[...] = y * _sigmoid(y)
    tail = ue_ref[pl.ds(off + tl, npv), :]
    ue_ref[pl.ds(off, npv), :] = tail

    @pl.when(l == pl.num_programs(1) - 1)
    def _():
        st_ref[...] = tail


def conv_module(h3, conv_prev, conv_w, conv_b, ln_g, ln_b, c_conv, *, tl=256):
    B, L, _ = h3.shape
    tl = _pick(L, tl)
    npv = CONV_W - 1
    row = lambda a: a.reshape(1, c_conv)
    return pl.pallas_call(
        functools.partial(_conv_kernel, tl=tl),
        grid=(B, L // tl),
        in_specs=[
            pl.BlockSpec((None, tl, c_conv), lambda b, l: (b, l, 0)),
            pl.BlockSpec((None, tl, c_conv), lambda b, l: (b, l, 1)),
            pl.BlockSpec((None, npv, c_conv), lambda b, l: (b, 0, 0)),
            pl.BlockSpec((CONV_W, c_conv), lambda b, l: (0, 0)),
            pl.BlockSpec((1, c_conv), lambda b, l: (0, 0)),
            pl.BlockSpec((1, c_conv), lambda b, l: (0, 0)),
            pl.BlockSpec((1, c_conv), lambda b, l: (0, 0)),
        ],
        out_specs=[
            pl.BlockSpec((None, tl, c_conv), lambda b, l: (b, l, 0)),
            pl.BlockSpec((None, npv, c_conv), lambda b, l: (b, 0, 0)),
        ],
        out_shape=[
            jax.ShapeDtypeStruct((B, L, c_conv), F32),
            jax.ShapeDtypeStruct((B, npv, c_conv), F32),
        ],
        scratch_shapes=[pltpu.VMEM((HALO + tl, c_conv), F32)],
        compiler_params=_params(("arbitrary", "arbitrary")),
    )(h3, h3, conv_prev, conv_w, row(conv_b), row(ln_g), row(ln_b))


def _t5_bucket(dist):
    n = jnp.maximum(dist, 0)
    max_exact = NUM_BUCKETS // 2
    nf = jnp.maximum(n, 1).astype(F32)
    large = max_exact + (jnp.log(nf / max_exact) / math.log(MAX_DISTANCE / max_exact)
                         * (NUM_BUCKETS - max_exact)).astype(jnp.int32)
    large = jnp.minimum(large, NUM_BUCKETS - 1)
    return jnp.where(n < max_exact, n, large)


def _bias_of_dist(rel_bias, dist):
    onehot = (_t5_bucket(dist)[..., None] == jnp.arange(NUM_BUCKETS)).astype(F32)
    b = jnp.moveaxis(jnp.dot(onehot, rel_bias, precision=HIGHEST), -1, 0)
    return jnp.where(dist[None] >= 0, b, NEG)


def _kmean_kernel(k_ref, o_ref, *, nb):
    k = k_ref[...]
    km = jnp.mean(k.reshape(nb, MOBA_BLOCK, HD_ATTN), axis=1)
    o_ref[...] = jnp.zeros_like(o_ref)
    o_ref[pl.ds(0, nb), :] = km


def _select_topk(gate, valid, axis):
    idx = lax.broadcasted_iota(jnp.int32, gate.shape, axis).astype(F32)
    big = 1e9
    g = jnp.where(valid, gate, -jnp.inf)
    sel = jnp.zeros(gate.shape, jnp.bool_)
    for _ in range(MOBA_TOPK):
        mx = jnp.max(g, axis=axis, keepdims=True)
        is_max = (g == mx) & (mx > -jnp.inf)
        first = jnp.min(jnp.where(is_max, idx, big), axis=axis, keepdims=True)
        pick = idx == first
        sel = sel | pick
        g = jnp.where(pick, -jnp.inf, g)
    return jnp.where(sel, 0.0, NEG).astype(F32)


FAR_GROUP = 4


def _flash_kernel(q_ref, k_ref, v_ref, km_ref, bd_ref, bp_ref, bf_ref, o_ref,
                  kb_ref, vt_ref, qs_ref, sel_ref, m_ref, l_ref, acc_ref, *, nb):
    i = pl.program_id(1)
    T = MOBA_BLOCK

    @pl.when(i == 0)
    def _():
        def convert(n, carry):
            r0 = pl.multiple_of(n * T, T)
            kb_ref[n] = k_ref[pl.ds(r0, T), :].astype(BF16)
            vt_ref[n] = v_ref[pl.ds(r0, T), :].T.astype(BF16)
            return carry

        lax.fori_loop(0, nb, convert, 0)

    q = q_ref[...]
    qs_ref[...] = (q * (HD_ATTN ** -0.5)).astype(BF16)
    gate_t = lax.dot_general(km_ref[...], q, (((1,), (1,)), ((), ())),
                             precision=HIGHEST, preferred_element_type=F32)
    blk = lax.broadcasted_iota(jnp.int32, gate_t.shape, 0)
    sel_ref[...] = _select_topk(gate_t, blk < i, 0)
    m_ref[...] = jnp.full_like(m_ref, NEG)
    l_ref[...] = jnp.zeros_like(l_ref)
    acc_ref[...] = jnp.zeros_like(acc_ref)

    def update(blocks):
        qs = qs_ref[...]
        ss = [lax.dot_general(kb_ref[j], qs, (((1,), (1,)), ((), ())), preferred_element_type=F32) + extra
              for j, extra in blocks]
        m_old = m_ref[...]
        m_new = m_old
        for s in ss:
            m_new = jnp.maximum(m_new, jnp.max(s, axis=0, keepdims=True))
        alpha = jnp.exp(m_old - m_new)
        l_new = alpha * l_ref[...]
        acc = alpha * acc_ref[...]
        for (j, _), s in zip(blocks, ss):
            p = jnp.exp(s - m_new)
            l_new = l_new + jnp.sum(p, axis=0, keepdims=True)
            acc = acc + jnp.dot(vt_ref[j], p.astype(BF16), preferred_element_type=F32)
        l_ref[...] = l_new
        acc_ref[...] = acc
        m_ref[...] = m_new

    def far_term(j):
        return bf_ref[...] + sel_ref[pl.ds(j, 1), :]

    n_far = jnp.maximum(i - 1, 0)
    n_groups = n_far // FAR_GROUP

    def far_group(g, carry):
        j0 = g * FAR_GROUP
        update([(j0 + b, far_term(j0 + b)) for b in range(FAR_GROUP)])
        return carry

    def far_single(j, carry):
        update([(j, far_term(j))])
        return carry

    lax.fori_loop(0, n_groups, far_group, 0)
    lax.fori_loop(n_groups * FAR_GROUP, n_far, far_single, 0)

    @pl.when(i == 0)
    def _():
        update([(i, bd_ref[...])])

    @pl.when(i >= 1)
    def _():
        update([(i - 1, bp_ref[...] + sel_ref[pl.ds(i - 1, 1), :]), (i, bd_ref[...])])

    o_ref[...] = (acc_ref[...] / l_ref[...]).T


def moba_prompt(h2, rel_bias, n_heads, q_col, k_col, v_col):
    L = h2.shape[0]
    nb = L // MOBA_BLOCK
    assert L % MOBA_BLOCK == 0 and nb <= LANES
    T = MOBA_BLOCK
    kmean = pl.pallas_call(
        functools.partial(_kmean_kernel, nb=nb),
        grid=(n_heads,),
        in_specs=[pl.BlockSpec((L, HD_ATTN), lambda h: (0, k_col + h))],
        out_specs=pl.BlockSpec((None, LANES, HD_ATTN), lambda h: (h, 0, 0)),
        out_shape=jax.ShapeDtypeStruct((n_heads, LANES, HD_ATTN), F32),
        compiler_params=_params(("parallel",)),
    )(h2)

    ki = jnp.arange(T)[:, None]
    qi = jnp.arange(T)[None, :]
    bias_diag = _bias_of_dist(rel_bias, qi - ki)
    bias_prev = _bias_of_dist(rel_bias, qi - ki + T)
    bias_far = jnp.broadcast_to(rel_bias[NUM_BUCKETS - 1][:, None, None], (n_heads, 1, T))

    return pl.pallas_call(
        functools.partial(_flash_kernel, nb=nb),
        grid=(n_heads, nb),
        in_specs=[
            pl.BlockSpec((T, HD_ATTN), lambda h, i: (i, q_col + h)),
            pl.BlockSpec((L, HD_ATTN), lambda h, i: (0, k_col + h)),
            pl.BlockSpec((L, HD_ATTN), lambda h, i: (0, v_col + h)),
            pl.BlockSpec((None, LANES, HD_ATTN), lambda h, i: (h, 0, 0)),
            pl.BlockSpec((None, T, T), lambda h, i: (h, 0, 0)),
            pl.BlockSpec((None, T, T), lambda h, i: (h, 0, 0)),
            pl.BlockSpec((None, 1, T), lambda h, i: (h, 0, 0)),
        ],
        out_specs=pl.BlockSpec((T, HD_ATTN), lambda h, i: (i, h)),
        out_shape=jax.ShapeDtypeStruct((L, n_heads * HD_ATTN), F32),
        scratch_shapes=[
            pltpu.VMEM((nb, T, HD_ATTN), BF16),
            pltpu.VMEM((nb, HD_ATTN, T), BF16),
            pltpu.VMEM((T, HD_ATTN), BF16),
            pltpu.VMEM((LANES, T), F32),
            pltpu.VMEM((1, T), F32),
            pltpu.VMEM((1, T), F32),
            pltpu.VMEM((HD_ATTN, T), F32),
        ],
        compiler_params=_params(("parallel", "arbitrary")),
    )(h2, h2, h2, kmean, bias_diag, bias_prev, bias_far)


QPAD = 8
SAMPLE_BLOCKS_PER_STEP = 2


def _head_rows(ref, h, n_heads):
    return ref[pl.ds(h, ref.shape[0] // n_heads, stride=n_heads), :]


def _sample_p1_kernel(pt_ref, *refs, bps, n_heads):
    n = pl.program_id(1)
    page_refs = refs[:2 * bps]
    w_ref, lg_ref, gate_ref = refs[2 * bps:]
    w = w_ref[...].astype(BF16)
    P = page_refs[0].shape[0] // n_heads
    C = w.shape[1]
    col_head = lax.broadcasted_iota(jnp.int32, (P, C), 1) // QPAD
    for blk in range(bps):
        gsum = jnp.zeros((1, C), F32)
        for half in range(2):
            ref = page_refs[2 * blk + half]
            s = jnp.zeros((P, C), F32)
            for h in range(n_heads):
                sh = jnp.dot(_head_rows(ref, h, n_heads).astype(BF16), w, preferred_element_type=F32)
                s = jnp.where(col_head == h, sh, s)
            gsum = gsum + jnp.sum(s, axis=0, keepdims=True)
            lg_ref[blk, pl.ds(half * P, P), :] = s * (HD_ATTN ** -0.5)
        gate_ref[pl.ds(n * bps + blk, 1), :] = gsum * (1.0 / MOBA_BLOCK)


def _sample_p2_kernel(lg_ref, gate_ref, kn_ref, w_ref, bl_ref, bf_ref, bo_ref, p_ref, pn_ref, *, nbp, n_heads):
    gate = gate_ref[...]
    selm = _select_topk(gate, jnp.ones(gate.shape, jnp.bool_), 0)
    s = lg_ref[...] + selm[:, None, :]
    blk = lax.broadcasted_iota(jnp.int32, (nbp, 1, 1), 0)
    s = s + jnp.where(blk == nbp - 1, bl_ref[...][None], bf_ref[...][None])
    w = w_ref[...].astype(BF16)
    col_head = lax.broadcasted_iota(jnp.int32, (QPAD, w.shape[1]), 1) // QPAD
    sn = jnp.zeros((QPAD, w.shape[1]), F32)
    for h in range(n_heads):
        sn = jnp.where(col_head == h, jnp.dot(_head_rows(kn_ref, h, n_heads).astype(BF16), w,
                                              preferred_element_type=F32), sn)
    sn = sn * (HD_ATTN ** -0.5) + bo_ref[...]
    m = jnp.maximum(jnp.max(jnp.max(s, axis=1), axis=0, keepdims=True), jnp.max(sn, axis=0, keepdims=True))
    p = jnp.exp(s - m[None])
    pn = jnp.exp(sn - m)
    den = jnp.sum(jnp.sum(p, axis=1), axis=0, keepdims=True) + jnp.sum(pn, axis=0, keepdims=True)
    inv = 1.0 / den
    p = p * inv[None]
    for blk in range(nbp):
        p_ref[blk] = p[blk].T
    pn_ref[...] = pn * inv


def _sample_p3_kernel(pt_ref, *refs, n_heads, bps):
    n = pl.program_id(1)
    page_refs = refs[:2 * bps]
    p_ref, pn_ref, vn_ref, o_ref, acc_ref = refs[2 * bps:]
    P = page_refs[0].shape[0] // n_heads

    head_rows = lambda h: pl.ds(h * QPAD, QPAD)

    @pl.when(n == 0)
    def _():
        pn = pn_ref[...].astype(BF16)
        for h in range(n_heads):
            full = lax.dot_general(pn, _head_rows(vn_ref, h, n_heads).astype(BF16), (((0,), (0,)), ((), ())),
                                   preferred_element_type=F32)
            acc_ref[h] = full[h * QPAD:(h + 1) * QPAD]

    for h in range(n_heads):
        acc = acc_ref[h]
        for blk in range(bps):
            for half in range(2):
                p = p_ref[blk, head_rows(h), pl.ds(half * P, P)].astype(BF16)
                acc = acc + jnp.dot(p, _head_rows(page_refs[2 * blk + half], h, n_heads).astype(BF16),
                                    preferred_element_type=F32)
        acc_ref[h] = acc

    @pl.when(n == pl.num_programs(1) - 1)
    def _():
        for h in range(n_heads):
            o_ref[:, pl.ds(h * HD_ATTN, HD_ATTN)] = acc_ref[h]


def moba_sample(q, k_new, v_new, cache_k, cache_v, layer, page_table, rel_bias):
    B, L, H, HD = q.shape
    page = cache_k.shape[2]
    n_pages = page_table.shape[1]
    past = n_pages * page
    assert MOBA_BLOCK == 2 * page and past % MOBA_BLOCK == 0 and L <= QPAD
    nbp = past // MOBA_BLOCK
    C = H * QPAD
    CH = H * HD
    qp = jnp.pad(q, ((0, 0), (0, QPAD - L), (0, 0), (0, 0)))
    wq = qp.transpose(0, 3, 2, 1).reshape(B, HD, C)

    qpos = past + jnp.arange(QPAD)
    kl = (nbp - 1) * MOBA_BLOCK + jnp.arange(MOBA_BLOCK)
    bias_last = _bias_of_dist(rel_bias, qpos[None, :] - kl[:, None])
    bias_last = jnp.moveaxis(bias_last, 0, 1).reshape(MOBA_BLOCK, C)
    bias_far = jnp.repeat(rel_bias[NUM_BUCKETS - 1], QPAD).reshape(1, C)
    kown = past + jnp.arange(QPAD)
    dist_own = qpos[None, :] - kown[:, None]
    ok = (jnp.arange(QPAD)[:, None] < L) & (jnp.arange(QPAD)[None, :] < L)
    bias_own = _bias_of_dist(rel_bias, jnp.where(ok, dist_own, -1))
    bias_own = jnp.moveaxis(bias_own, 0, 1).reshape(QPAD, C)

    rows_ph = lambda t: t.reshape(t.shape[:-3] + (t.shape[-3] * H, HD))
    knp = rows_ph(jnp.pad(k_new, ((0, 0), (0, QPAD - L), (0, 0), (0, 0))))
    vnp = rows_ph(jnp.pad(v_new, ((0, 0), (0, QPAD - L), (0, 0), (0, 0))))
    cache_k, cache_v = rows_ph(cache_k), rows_ph(cache_v)

    bps = _pick(nbp, SAMPLE_BLOCKS_PER_STEP)
    page_specs = [pl.BlockSpec((None, None, page * H, HD),
                               functools.partial(lambda b, n, pt, j: (layer, pt[b, 2 * bps * n + j], 0, 0), j=j))
                  for j in range(2 * bps)]
    logits, gate = pl.pallas_call(
        functools.partial(_sample_p1_kernel, bps=bps, n_heads=H),
        grid_spec=pltpu.PrefetchScalarGridSpec(
            num_scalar_prefetch=1,
            grid=(B, nbp // bps),
            in_specs=page_specs + [pl.BlockSpec((None, HD, C), lambda b, n, pt: (b, 0, 0))],
            out_specs=[pl.BlockSpec((None, bps, MOBA_BLOCK, C), lambda b, n, pt: (b, n, 0, 0)),
                       pl.BlockSpec((None, nbp, C), lambda b, n, pt: (b, 0, 0))],
        ),
        out_shape=[jax.ShapeDtypeStruct((B, nbp, MOBA_BLOCK, C), F32),
                   jax.ShapeDtypeStruct((B, nbp, C), F32)],
        compiler_params=_params(("parallel", "arbitrary")),
    )(page_table, *([cache_k] * (2 * bps)), wq)

    p_past, p_new = pl.pallas_call(
        functools.partial(_sample_p2_kernel, nbp=nbp, n_heads=H),
        grid=(B,),
        in_specs=[
            pl.BlockSpec((None, nbp, MOBA_BLOCK, C), lambda b: (b, 0, 0, 0)),
            pl.BlockSpec((None, nbp, C), lambda b: (b, 0, 0)),
            pl.BlockSpec((None, QPAD * H, HD), lambda b: (b, 0, 0)),
            pl.BlockSpec((None, HD, C), lambda b: (b, 0, 0)),
            pl.BlockSpec((MOBA_BLOCK, C), lambda b: (0, 0)),
            pl.BlockSpec((1, C), lambda b: (0, 0)),
            pl.BlockSpec((QPAD, C), lambda b: (0, 0)),
        ],
        out_specs=[pl.BlockSpec((None, nbp, C, MOBA_BLOCK), lambda b: (b, 0, 0, 0)),
                   pl.BlockSpec((None, QPAD, C), lambda b: (b, 0, 0))],
        out_shape=[jax.ShapeDtypeStruct((B, nbp, C, MOBA_BLOCK), F32),
                   jax.ShapeDtypeStruct((B, QPAD, C), F32)],
        compiler_params=_params(("parallel",)),
    )(logits, gate, knp, wq, bias_last, bias_far, bias_own)

    o = pl.pallas_call(
        functools.partial(_sample_p3_kernel, n_heads=H, bps=bps),
        grid_spec=pltpu.PrefetchScalarGridSpec(
            num_scalar_prefetch=1,
            grid=(B, nbp // bps),
            in_specs=page_specs + [
                pl.BlockSpec((None, bps, C, MOBA_BLOCK), lambda b, n, pt: (b, n, 0, 0)),
                pl.BlockSpec((None, QPAD, C), lambda b, n, pt: (b, 0, 0)),
                pl.BlockSpec((None, QPAD * H, HD), lambda b, n, pt: (b, 0, 0))],
            out_specs=pl.BlockSpec((None, QPAD, CH), lambda b, n, pt: (b, 0, 0)),
            scratch_shapes=[pltpu.VMEM((H, QPAD, HD), F32)],
        ),
        out_shape=jax.ShapeDtypeStruct((B, QPAD, CH), F32),
        compiler_params=_params(("parallel", "arbitrary")),
    )(page_table, *([cache_v] * (2 * bps)), p_past, p_new, vnp)
    return o[:, :L]


def _head_minor(x, n_heads, axis=-1):
    axis = axis % x.ndim
    shp = x.shape
    x = x.reshape(shp[:axis] + (n_heads, shp[axis] // n_heads) + shp[axis + 1:])
    return jnp.swapaxes(x, axis, axis + 1).reshape(shp)


def _head_sum(x, n_heads):
    nt = x.shape[1] // LANES
    t = x[:, 0:LANES]
    for c in range(1, nt):
        t = t + x[:, c * LANES:(c + 1) * LANES]
    shift = n_heads
    while shift < LANES:
        t = t + pltpu.roll(t, shift, axis=1)
        shift *= 2
    return jnp.concatenate([t] * nt, axis=1)


def _rwkv_prep_kernel(k_ref, a_ref, kk_ref, ka_ref, k2_ref, na_ref, nb_ref, *, n_heads):
    k = k_ref[...]
    a = a_ref[...]
    kk = k * kk_ref[...]
    nrm = jnp.maximum(jnp.sqrt(_head_sum(kk * kk, n_heads)), 1e-12)
    kk = kk / nrm
    k2_ref[...] = k * (1.0 + (a - 1.0) * ka_ref[...])
    na_ref[...] = -kk
    nb_ref[...] = kk * a


def rwkv_prep(k, a, k_k, k_a, n_heads, *, tm=256):
    M, D = k.shape
    tm = _pick(M, tm)
    blk = pl.BlockSpec((tm, D), lambda i: (i, 0))
    row = pl.BlockSpec((1, D), lambda i: (0, 0))
    return pl.pallas_call(
        functools.partial(_rwkv_prep_kernel, n_heads=n_heads),
        grid=(M // tm,),
        in_specs=[blk, blk, row, row],
        out_specs=[blk, blk, blk],
        out_shape=[jax.ShapeDtypeStruct((M, D), F32)] * 3,
        compiler_params=_params(("parallel",)),
    )(k, a, k_k.reshape(1, D), k_a.reshape(1, D))


def _rwkv_post_kernel(y_ref, r_ref, k2_ref, v_ref, g_ref, rk_ref, lg_ref, lb_ref, o_ref, *, n_heads):
    y = y_ref[...]
    inv = 1.0 / HS_RWKV
    mu = _head_sum(y, n_heads) * inv
    yc = y - mu
    var = _head_sum(yc * yc, n_heads) * inv
    yn = yc * lax.rsqrt(var + GN_EPS) * lg_ref[...] + lb_ref[...]
    bonus = _head_sum(r_ref[...] * k2_ref[...] * rk_ref[...], n_heads) * v_ref[...]
    o_ref[...] = (yn + bonus) * g_ref[...]


def rwkv_post(y, r, k2, v, g, r_k, lnx_g, lnx_b, n_heads, *, tm=256):
    M, D = y.shape
    tm = _pick(M, tm)
    blk = pl.BlockSpec((tm, D), lambda i: (i, 0))
    row = pl.BlockSpec((1, D), lambda i: (0, 0))
    return pl.pallas_call(
        functools.partial(_rwkv_post_kernel, n_heads=n_heads),
        grid=(M // tm,),
        in_specs=[blk] * 5 + [row] * 3,
        out_specs=blk,
        out_shape=jax.ShapeDtypeStruct((M, D), F32),
        compiler_params=_params(("parallel",)),
    )(y, r, k2, v, g, r_k.reshape(1, D), lnx_g.reshape(1, D), lnx_b.reshape(1, D))


def _split3(x):
    hi = x.astype(BF16)
    r1 = x - hi.astype(F32)
    mid = r1.astype(BF16)
    lo = (r1 - mid.astype(F32)).astype(BF16)
    return hi, mid, lo


SCAN_GROUP = 16


def _scan_kernel(r_ref, d_ref, k_ref, a_ref, b_ref, v_ref, s0_ref, y_ref, so_ref, cola_ref, colb_ref, s_ref, vy_ref,
                 *, ng, nh, nvh, n_valid):
    c = pl.program_id(1)
    G = SCAN_GROUP
    klo = LANES // nh
    nt = HS_RWKV // klo

    @pl.when(c == 0)
    def _():
        s_ref[...] = s0_ref[...]

    rr = lax.broadcasted_iota(jnp.int32, (LANES, klo * LANES), 0)
    cc = lax.broadcasted_iota(jnp.int32, (LANES, klo * LANES), 1)
    spread = ((rr // nh == cc // LANES) & (rr % nh == cc % nh)).astype(BF16)
    spread3 = jnp.concatenate([spread, spread, spread], axis=0)
    ops = (r_ref, d_ref, k_ref, a_ref, b_ref)

    def build(gi, col_ref):
        rows = pl.ds(pl.multiple_of(gi * G, G), G)
        pieces = []
        for ref in ops:
            hi, mid, lo = _split3(ref[rows, :])
            for jt in range(nt):
                ls = slice(jt * LANES, (jt + 1) * LANES)
                pieces.append(jnp.concatenate([hi[:, ls], mid[:, ls], lo[:, ls]], axis=1))
        out = jnp.dot(jnp.concatenate(pieces, axis=0), spread3, preferred_element_type=F32)
        for o in range(len(ops)):
            for jt in range(nt):
                src = (o * nt + jt) * G
                for kl in range(klo):
                    dst = (o * HS_RWKV + jt * klo + kl) * G
                    col_ref[pl.ds(dst, G), :] = out[src:src + G, kl * LANES:(kl + 1) * LANES]

    def consume(gi, col_ref, ntok):
        rows = pl.ds(pl.multiple_of(gi * G, G), G)
        vy_ref[...] = v_ref[rows, :]
        for tt in range(ntok):
            tile = lambda o: col_ref[pl.ds(o * HS_RWKV * G + tt, HS_RWKV, stride=G), :]
            r_c, d_c, k_c, a_c, b_c = (tile(o) for o in range(5))
            for vh in range(nvh):
                lanes = pl.ds(vh * LANES, LANES)
                S = s_ref[vh]
                sa = jnp.sum(S * a_c, axis=0, keepdims=True)
                S = S * d_c + sa * b_c + vy_ref[pl.ds(tt, 1), lanes] * k_c
                s_ref[vh] = S
                vy_ref[pl.ds(tt, 1), lanes] = jnp.sum(S * r_c, axis=0, keepdims=True)
        y_ref[rows, :] = vy_ref[...]

    build(0, cola_ref)
    if ng % 2:
        assert ng == 1
        consume(0, cola_ref, n_valid)
    else:
        assert n_valid == G
        def pair(gp, carry):
            g0 = 2 * gp
            build(g0 + 1, colb_ref)
            consume(g0, cola_ref, G)
            build(jnp.minimum(g0 + 2, ng - 1), cola_ref)
            consume(g0 + 1, colb_ref, G)
            return carry

        lax.fori_loop(0, ng // 2, pair, 0)

    @pl.when(c == pl.num_programs(1) - 1)
    def _():
        so_ref[...] = s_ref[...]


def rwkv_scan(r, d, k, a, b, v, state, n_heads, *, tc=64):
    B, L, D = r.shape
    H = n_heads
    G = SCAN_GROUP
    vlo = LANES // H
    nvh = HS_RWKV // vlo
    assert H * vlo == LANES and nvh * vlo == HS_RWKV
    Lp = -(-L // G) * G
    n_valid = G
    if Lp != L:
        assert Lp == G
        n_valid = L
        r, d, k, a, b, v = (jnp.pad(t, ((0, 0), (0, Lp - L), (0, 0))) for t in (r, d, k, a, b, v))
    tc = _pick(Lp, tc)
    ng = tc // G
    s0 = state.reshape(B, H, nvh, vlo, HS_RWKV).transpose(0, 2, 4, 3, 1).reshape(B, nvh, HS_RWKV, LANES)
    xspec = pl.BlockSpec((None, tc, D), lambda bb, c: (bb, c, 0))
    sspec = pl.BlockSpec((None, nvh, HS_RWKV, LANES), lambda bb, c: (bb, 0, 0, 0))
    y, s1 = pl.pallas_call(
        functools.partial(_scan_kernel, ng=ng, nh=H, nvh=nvh, n_valid=n_valid),
        grid=(B, Lp // tc),
        in_specs=[xspec] * 6 + [sspec],
        out_specs=[xspec, sspec],
        out_shape=[jax.ShapeDtypeStruct((B, Lp, D), F32),
                   jax.ShapeDtypeStruct((B, nvh, HS_RWKV, LANES), F32)],
        scratch_shapes=[pltpu.VMEM((5 * HS_RWKV * G, LANES), F32),
                        pltpu.VMEM((5 * HS_RWKV * G, LANES), F32),
                        pltpu.VMEM((nvh, HS_RWKV, LANES), F32),
                        pltpu.VMEM((G, D), F32)],
        compiler_params=_params(("parallel", "arbitrary")),
    )(r, d, k, a, b, v, s0)
    s1 = s1.reshape(B, nvh, HS_RWKV, vlo, H).transpose(0, 4, 1, 3, 2).reshape(B, H, HS_RWKV, HS_RWKV)
    return y[:, :L], s1


def _conv_attn_layer(x, conv_prev, attn_fn, w_in, conv_w, conv_b, cln_g, cln_b, w_out, ln_g, ln_b, alpha,
                     c_conv, c_attn):
    B, L, D = x.shape
    x2 = x.reshape(B * L, D)
    h = matmul(x2, w_in).reshape(B, L, -1)
    c, cstate = conv_module(h, conv_prev, conv_w, conv_b, cln_g, cln_b, c_conv)
    o = attn_fn(h)
    co = jnp.concatenate([c, o], axis=-1).reshape(B * L, c_conv + c_attn)
    xn = rowmm_ln(co, w_out, x2, ln_g, ln_b, alpha).reshape(B, L, D)
    k = h[..., 2 * c_conv + c_attn:2 * c_conv + 2 * c_attn]
    v = h[..., 2 * c_conv + 2 * c_attn:]
    return xn, k, v, cstate


def _rwkv_layer(x, shift_prev, wkv_prev, p, ln_g, ln_b, alpha, n_heads):
    B, L, D = x.shape
    x_prev = jnp.concatenate([shift_prev[:, None], x[:, :-1]], axis=1).reshape(B * L, D)
    x2 = x.reshape(B * L, D)
    mixrow = lambda i: p["mix"][i].reshape(1, D)
    proj = lambda i, w, **kw: matmul(x2, w, xprev=x_prev, mixrow=mixrow(i), **kw)
    r = proj(0, p["wr"])
    lw = proj(1, p["w1"], epilogue="tanh")
    d = matmul(lw, p["w2"], bias=p["w0"].reshape(1, D), epilogue="decay")
    k = proj(2, p["wk"])
    v = proj(3, p["wv"])
    la = proj(4, p["a1"])
    a = matmul(la, p["a2"], bias=p["a0"].reshape(1, D), epilogue="sigmoid")
    lg = proj(5, p["g1"], epilogue="sigmoid")
    g = matmul(lg, p["g2"])
    k2, na, nb = rwkv_prep(k, a, p["kk"], p["ka"], n_heads)
    sh = lambda t: t.reshape(B, L, D)
    y, s1 = rwkv_scan(sh(r), sh(d), sh(k2), sh(na), sh(nb), sh(v), wkv_prev, n_heads)
    yg = rwkv_post(y.reshape(B * L, D), r, k2, v, g, p["rk"], p["lnx_g"], p["lnx_b"], n_heads)
    xn = rowmm_ln(yg, p["wo"], x2, ln_g, ln_b, alpha).reshape(B, L, D)
    return xn, s1, x[:, -1]


def kernel(x_prompt, x_sample, cache_k, cache_v, page_table, state_conv, state_wkv, state_shift, rel_bias, ab_w_in, ab_conv_w, ab_conv_b, ab_ln_g, ab_ln_b, ab_w_out, rw_mix, rw_wr, rw_wk, rw_wv, rw_wo, rw_w0, rw_w1, rw_w2, rw_a0, rw_a1, rw_a2, rw_g1, rw_g2, rw_kk, rw_ka, rw_rk, rw_lnx_g, rw_lnx_b, ln_g, ln_b, mlp_w1, mlp_w2):
    B, L, D = x_prompt.shape
    DB, DL, _ = x_sample.shape
    depth = ln_g.shape[0]
    alpha = (2 * depth) ** 0.25
    c_conv = ab_conv_w.shape[2]
    n_heads_attn = cache_k.shape[3]
    c_attn = n_heads_attn * HD_ATTN
    n_heads_rwkv = D // HS_RWKV
    assert B == 1
    bf = lambda w: w.astype(BF16)
    row = lambda a: a.reshape(1, -1)
    q_col = 2 * c_conv // LANES
    k_col = q_col + c_attn // LANES
    v_col = k_col + c_attn // LANES

    xp, xs = x_prompt, x_sample
    outs = {n: [] for n in ("kp", "vp", "ks", "vs", "cp", "cs", "wp", "ws", "hp", "hs")}
    for layer in range(depth):
        i = layer // 2
        g0, b0, g1, b1 = row(ln_g[layer, 0]), row(ln_b[layer, 0]), row(ln_g[layer, 1]), row(ln_b[layer, 1])
        if layer % 2 == 0:
            ab = (bf(ab_w_in[i]), ab_conv_w[i], ab_conv_b[i], ab_ln_g[i], ab_ln_b[i], bf(ab_w_out[i]), g0, b0, alpha,
                  c_conv, c_attn)
            attn_p = lambda h: moba_prompt(h.reshape(L, -1), rel_bias, n_heads_attn, q_col, k_col, v_col)[None]
            xp, kp, vp, cp = _conv_attn_layer(xp, jnp.zeros((B, CONV_W - 1, c_conv), F32), attn_p, *ab)

            def attn_s(h):
                hd = lambda lo: h[..., lo:lo + c_attn].reshape(DB, DL, n_heads_attn, HD_ATTN)
                return moba_sample(hd(2 * c_conv), hd(2 * c_conv + c_attn), hd(2 * c_conv + 2 * c_attn),
                                   cache_k, cache_v, i, page_table, rel_bias)

            xs, ks, vs, cs = _conv_attn_layer(xs, state_conv[i], attn_s, *ab)
            hd4 = lambda t: t.reshape(t.shape[0], t.shape[1], n_heads_attn, HD_ATTN)
            outs["kp"].append(hd4(kp)); outs["vp"].append(hd4(vp))
            outs["ks"].append(hd4(ks)); outs["vs"].append(hd4(vs))
            outs["cp"].append(cp); outs["cs"].append(cs)
        else:
            hm = functools.partial(_head_minor, n_heads=n_heads_rwkv)
            p = dict(mix=rw_mix[i], wr=bf(hm(rw_wr[i])), wk=bf(hm(rw_wk[i])), wv=bf(hm(rw_wv[i])),
                     wo=bf(hm(rw_wo[i], axis=0)), w0=hm(rw_w0[i]), w1=bf(rw_w1[i]), w2=bf(hm(rw_w2[i])),
                     a0=hm(rw_a0[i]), a1=bf(rw_a1[i]), a2=bf(hm(rw_a2[i])), g1=bf(rw_g1[i]), g2=bf(hm(rw_g2[i])),
                     kk=hm(rw_kk[i]), ka=hm(rw_ka[i]), rk=hm(rw_rk[i].reshape(-1)),
                     lnx_g=hm(rw_lnx_g[i]), lnx_b=hm(rw_lnx_b[i]))
            xp, wp, hp = _rwkv_layer(xp, jnp.zeros((B, D), F32),
                                     jnp.zeros((B, n_heads_rwkv, HS_RWKV, HS_RWKV), F32), p, g0, b0, alpha,
                                     n_heads_rwkv)
            xs, ws, hs = _rwkv_layer(xs, state_shift[i], state_wkv[i], p, g0, b0, alpha, n_heads_rwkv)
            outs["wp"].append(wp); outs["ws"].append(ws); outs["hp"].append(hp); outs["hs"].append(hs)
        w1b, w2b = bf(mlp_w1[layer]), bf(mlp_w2[layer])
        xp = mlp_ln(xp.reshape(B * L, D), w1b, w2b, g1, b1, alpha).reshape(B, L, D)
        xs = mlp_ln(xs.reshape(DB * DL, D), w1b, w2b, g1, b1, alpha).reshape(DB, DL, D)
    st = lambda n: jnp.stack(outs[n])
    return (xp, xs, st("kp"), st("vp"), st("ks"), st("vs"), st("cp"), st("cs"),
            st("wp"), st("ws"), st("hp"), st("hs"))
```

```python
import functools
import math

import jax
import jax.numpy as jnp
import numpy as np
from jax import lax
from jax.experimental import pallas as pl
from jax.experimental.pallas import tpu as pltpu

F32 = jnp.float32
BF16 = jnp.bfloat16
HIGHEST = lax.Precision.HIGHEST

LANES = 128
SUBLANES = 8
VMEM_LIMIT = 56 * 1024 * 1024

CONV_W = 31
HD_ATTN = 128
MOBA_BLOCK = 256
MOBA_TOPK = 3
NUM_BUCKETS = 32
MAX_DISTANCE = 128
HS_RWKV = 64
GN_EPS = 64e-5
LN_EPS = 1e-5
NEG = -1e30


def _params(sem):
    return pltpu.CompilerParams(dimension_semantics=sem, vmem_limit_bytes=VMEM_LIMIT)


def _pick(n, pref):
    if n <= pref:
        return n
    t = pref
    while n % t:
        t //= 2
    return t


def _softplus(y):
    return jnp.maximum(y, 0.0) + jnp.log(1.0 + jnp.exp(-jnp.abs(y)))


def _sigmoid(y):
    return 1.0 / (1.0 + jnp.exp(-y))


def _epilogue(acc, kind):
    if kind == "none":
        return acc
    if kind == "tanh":
        return jnp.tanh(acc)
    if kind == "sigmoid":
        return _sigmoid(acc)
    if kind == "decay":
        return jnp.exp(-jnp.exp(-_softplus(-acc) - 0.5))
    raise ValueError(kind)


def _mm_kernel(*refs, mix, bias, epilogue):
    it = iter(refs)
    x_ref = next(it)
    xp_ref = next(it) if mix else None
    m_ref = next(it) if mix else None
    w_ref = next(it)
    b_ref = next(it) if bias else None
    o_ref = next(it)
    xs_ref = next(it)

    @pl.when(pl.program_id(1) == 0)
    def _():
        x = x_ref[...]
        if mix:
            x = x + (xp_ref[...] - x) * m_ref[...]
        xs_ref[...] = x.astype(BF16)

    acc = jnp.dot(xs_ref[...], w_ref[...], preferred_element_type=F32)
    if bias:
        acc = acc + b_ref[...]
    o_ref[...] = _epilogue(acc, epilogue)


def matmul(x, w, *, xprev=None, mixrow=None, bias=None, epilogue="none", tm=512, tn=512):
    M, K = x.shape
    N = w.shape[1]
    tm = _pick(M, tm)
    tn = _pick(N, tn)
    mix = xprev is not None
    ins = [x]
    specs = [pl.BlockSpec((tm, K), lambda i, j: (i, 0))]
    if mix:
        ins += [xprev, mixrow]
        specs += [pl.BlockSpec((tm, K), lambda i, j: (i, 0)), pl.BlockSpec((1, K), lambda i, j: (0, 0))]
    ins.append(w)
    specs.append(pl.BlockSpec((K, tn), lambda i, j: (0, j)))
    if bias is not None:
        ins.append(bias)
        specs.append(pl.BlockSpec((1, tn), lambda i, j: (0, j)))
    return pl.pallas_call(
        functools.partial(_mm_kernel, mix=mix, bias=bias is not None, epilogue=epilogue),
        grid=(M // tm, N // tn),
        in_specs=specs,
        out_specs=pl.BlockSpec((tm, tn), lambda i, j: (i, j)),
        out_shape=jax.ShapeDtypeStruct((M, N), F32),
        scratch_shapes=[pltpu.VMEM((tm, K), BF16)],
        compiler_params=_params(("parallel", "arbitrary")),
    )(*ins)


def _layer_norm(z, g, b):
    mu = jnp.mean(z, axis=-1, keepdims=True)
    zc = z - mu
    var = jnp.mean(zc * zc, axis=-1, keepdims=True)
    return zc * lax.rsqrt(var + LN_EPS) * g + b


def _rowmm_ln_kernel(x_ref, w_ref, r_ref, g_ref, b_ref, o_ref, acc_ref, *, alpha):
    k = pl.program_id(1)

    @pl.when(k == 0)
    def _():
        acc_ref[...] = jnp.zeros_like(acc_ref)

    acc_ref[...] += jnp.dot(x_ref[...].astype(BF16), w_ref[...], preferred_element_type=F32)

    @pl.when(k == pl.num_programs(1) - 1)
    def _():
        o_ref[...] = _layer_norm(alpha * r_ref[...] + acc_ref[...], g_ref[...], b_ref[...])


def rowmm_ln(x, w, resid, g, b, alpha, *, tm=512, tk=2048):
    M, K = x.shape
    N = w.shape[1]
    tm = _pick(M, tm)
    tk = _pick(K, tk)
    return pl.pallas_call(
        functools.partial(_rowmm_ln_kernel, alpha=alpha),
        grid=(M // tm, K // tk),
        in_specs=[
            pl.BlockSpec((tm, tk), lambda i, k: (i, k)),
            pl.BlockSpec((tk, N), lambda i, k: (k, 0)),
            pl.BlockSpec((tm, N), lambda i, k: (i, 0)),
            pl.BlockSpec((1, N), lambda i, k: (0, 0)),
            pl.BlockSpec((1, N), lambda i, k: (0, 0)),
        ],
        out_specs=pl.BlockSpec((tm, N), lambda i, k: (i, 0)),
        out_shape=jax.ShapeDtypeStruct((M, N), F32),
        scratch_shapes=[pltpu.VMEM((tm, N), F32)],
        compiler_params=_params(("parallel", "arbitrary")),
    )(x, w, resid, g, b)


def _mlp_ln_kernel(x_ref, w1_ref, w2_ref, g_ref, b_ref, o_ref, xs_ref, acc_ref, *, alpha):
    f = pl.program_id(1)

    @pl.when(f == 0)
    def _():
        xs_ref[...] = x_ref[...].astype(BF16)
        acc_ref[...] = jnp.zeros_like(acc_ref)

    h = jnp.dot(xs_ref[...], w1_ref[...], preferred_element_type=F32)
    h = jnp.maximum(h, 0.0)
    h = (h * h).astype(BF16)
    acc_ref[...] += jnp.dot(h, w2_ref[...], preferred_element_type=F32)

    @pl.when(f == pl.num_programs(1) - 1)
    def _():
        o_ref[...] = _layer_norm(alpha * x_ref[...] + acc_ref[...], g_ref[...], b_ref[...])


def mlp_ln(x, w1, w2, g, b, alpha, *, tm=1024, tf=512):
    M, D = x.shape
    FF = w1.shape[1]
    tm = _pick(M, tm)
    tf = _pick(FF, tf)
    once = pl.Buffered(1)
    return pl.pallas_call(
        functools.partial(_mlp_ln_kernel, alpha=alpha),
        grid=(M // tm, FF // tf),
        in_specs=[
            pl.BlockSpec((tm, D), lambda i, f: (i, 0), pipeline_mode=once),
            pl.BlockSpec((D, tf), lambda i, f: (0, f)),
            pl.BlockSpec((tf, D), lambda i, f: (f, 0)),
            pl.BlockSpec((1, D), lambda i, f: (0, 0)),
            pl.BlockSpec((1, D), lambda i, f: (0, 0)),
        ],
        out_specs=pl.BlockSpec((tm, D), lambda i, f: (i, 0)),
        out_shape=jax.ShapeDtypeStruct((M, D), F32),
        scratch_shapes=[pltpu.VMEM((tm, D), BF16), pltpu.VMEM((tm, D), F32)],
        compiler_params=_params(("parallel", "arbitrary")),
    )(x, w1, w2, g, b)


HALO = 32


def _conv_kernel(hv_ref, hg_ref, prev_ref, w_ref, cb_ref, g_ref, b_ref, c_ref, st_ref, ue_ref, *, tl):
    l = pl.program_id(1)
    npv = CONV_W - 1
    off = HALO - npv

    @pl.when(l == 0)
    def _():
        ue_ref[pl.ds(off, npv), :] = prev_ref[...]

    u = hv_ref[...] * _sigmoid(hg_ref[...])
    ue_ref[pl.ds(HALO, tl), :] = u
    acc = jnp.zeros_like(u) + cb_ref[...]
    for j in range(CONV_W):
        acc = acc + ue_ref[pl.ds(off + j, tl), :] * w_ref[pl.ds(j, 1), :]
    y = _layer_norm(acc, g_ref[...], b_ref[...])
    c_ref[...] = y * _sigmoid(y)
    tail = ue_ref[pl.ds(off + tl, npv), :]
    ue_ref[pl.ds(off, npv), :] = tail

    @pl.when(l == pl.num_programs(1) - 1)
    def _():
        st_ref[...] = tail


def conv_module(h3, conv_prev, conv_w, conv_b, ln_g, ln_b, c_conv, *, tl=256):
    B, L, _ = h3.shape
    tl = _pick(L, tl)
    npv = CONV_W - 1
    row = lambda a: a.reshape(1, c_conv)
    return pl.pallas_call(
        functools.partial(_conv_kernel, tl=tl),
        grid=(B, L // tl),
        in_specs=[
            pl.BlockSpec((None, tl, c_conv), lambda b, l: (b, l, 0)),
            pl.BlockSpec((None, tl, c_conv), lambda b, l: (b, l, 1)),
            pl.BlockSpec((None, npv, c_conv), lambda b, l: (b, 0, 0)),
            pl.BlockSpec((CONV_W, c_conv), lambda b, l: (0, 0)),
            pl.BlockSpec((1, c_conv), lambda b, l: (0, 0)),
            pl.BlockSpec((1, c_conv), lambda b, l: (0, 0)),
            pl.BlockSpec((1, c_conv), lambda b, l: (0, 0)),
        ],
        out_specs=[
            pl.BlockSpec((None, tl, c_conv), lambda b, l: (b, l, 0)),
            pl.BlockSpec((None, npv, c_conv), lambda b, l: (b, 0, 0)),
        ],
        out_shape=[
            jax.ShapeDtypeStruct((B, L, c_conv), F32),
            jax.ShapeDtypeStruct((B, npv, c_conv), F32),
        ],
        scratch_shapes=[pltpu.VMEM((HALO + tl, c_conv), F32)],
        compiler_params=_params(("arbitrary", "arbitrary")),
    )(h3, h3, conv_prev, conv_w, row(conv_b), row(ln_g), row(ln_b))


def _t5_bucket(dist):
    n = jnp.maximum(dist, 0)
    max_exact = NUM_BUCKETS // 2
    nf = jnp.maximum(n, 1).astype(F32)
    large = max_exact + (jnp.log(nf / max_exact) / math.log(MAX_DISTANCE / max_exact)
                         * (NUM_BUCKETS - max_exact)).astype(jnp.int32)
    large = jnp.minimum(large, NUM_BUCKETS - 1)
    return jnp.where(n < max_exact, n, large)


def _bias_of_dist(rel_bias, dist):
    onehot = (_t5_bucket(dist)[..., None] == jnp.arange(NUM_BUCKETS)).astype(F32)
    b = jnp.moveaxis(jnp.dot(onehot, rel_bias, precision=HIGHEST), -1, 0)
    return jnp.where(dist[None] >= 0, b, NEG)


def _kmean_kernel(k_ref, o_ref, *, nb):
    k = k_ref[...]
    km = jnp.mean(k.reshape(nb, MOBA_BLOCK, HD_ATTN), axis=1)
    o_ref[...] = jnp.zeros_like(o_ref)
    o_ref[pl.ds(0, nb), :] = km


def _select_topk(gate, valid, axis):
    idx = lax.broadcasted_iota(jnp.int32, gate.shape, axis).astype(F32)
    big = 1e9
    g = jnp.where(valid, gate, -jnp.inf)
    sel = jnp.zeros(gate.shape, jnp.bool_)
    for _ in range(MOBA_TOPK):
        mx = jnp.max(g, axis=axis, keepdims=True)
        is_max = (g == mx) & (mx > -jnp.inf)
        first = jnp.min(jnp.where(is_max, idx, big), axis=axis, keepdims=True)
        pick = idx == first
        sel = sel | pick
        g = jnp.where(pick, -jnp.inf, g)
    return jnp.where(sel, 0.0, NEG).astype(F32)


FAR_GROUP = 4


def _flash_kernel(q_ref, k_ref, v_ref, km_ref, bd_ref, bp_ref, bf_ref, o_ref,
                  kb_ref, vt_ref, qs_ref, sel_ref, m_ref, l_ref, acc_ref, *, nb):
    i = pl.program_id(1)
    T = MOBA_BLOCK

    @pl.when(i == 0)
    def _():
        def convert(n, carry):
            r0 = pl.multiple_of(n * T, T)
            kb_ref[n] = k_ref[pl.ds(r0, T), :].astype(BF16)
            vt_ref[n] = v_ref[pl.ds(r0, T), :].T.astype(BF16)
            return carry

        lax.fori_loop(0, nb, convert, 0)

    q = q_ref[...]
    qs_ref[...] = (q * (HD_ATTN ** -0.5)).astype(BF16)
    gate_t = lax.dot_general(km_ref[...], q, (((1,), (1,)), ((), ())),
                             precision=HIGHEST, preferred_element_type=F32)
    blk = lax.broadcasted_iota(jnp.int32, gate_t.shape, 0)
    sel_ref[...] = _select_topk(gate_t, blk < i, 0)
    m_ref[...] = jnp.full_like(m_ref, NEG)
    l_ref[...] = jnp.zeros_like(l_ref)
    acc_ref[...] = jnp.zeros_like(acc_ref)

    def update(blocks):
        qs = qs_ref[...]
        ss = [lax.dot_general(kb_ref[j], qs, (((1,), (1,)), ((), ())), preferred_element_type=F32) + extra
              for j, extra in blocks]
        m_old = m_ref[...]
        m_new = m_old
        for s in ss:
            m_new = jnp.maximum(m_new, jnp.max(s, axis=0, keepdims=True))
        alpha = jnp.exp(m_old - m_new)
        l_new = alpha * l_ref[...]
        acc = alpha * acc_ref[...]
        for (j, _), s in zip(blocks, ss):
            p = jnp.exp(s - m_new)
            l_new = l_new + jnp.sum(p, axis=0, keepdims=True)
            acc = acc + jnp.dot(vt_ref[j], p.astype(BF16), preferred_element_type=F32)
        l_ref[...] = l_new
        acc_ref[...] = acc
        m_ref[...] = m_new

    def far_term(j):
        return bf_ref[...] + sel_ref[pl.ds(j, 1), :]

    n_far = jnp.maximum(i - 1, 0)
    n_pairs = n_far // (2 * FAR_GROUP)
    n_groups = n_far // FAR_GROUP

    def group_blocks(g):
        return [(g * FAR_GROUP + b, far_term(g * FAR_GROUP + b)) for b in range(FAR_GROUP)]

    def far_pair(gp, carry):
        update(group_blocks(2 * gp))
        update(group_blocks(2 * gp + 1))
        return carry

    def far_group(g, carry):
        update(group_blocks(g))
        return carry

    def far_single(j, carry):
        update([(j, far_term(j))])
        return carry

    lax.fori_loop(0, n_pairs, far_pair, 0)
    lax.fori_loop(2 * n_pairs, n_groups, far_group, 0)
    lax.fori_loop(n_groups * FAR_GROUP, n_far, far_single, 0)

    @pl.when(i == 0)
    def _():
        update([(i, bd_ref[...])])

    @pl.when(i >= 1)
    def _():
        update([(i - 1, bp_ref[...] + sel_ref[pl.ds(i - 1, 1), :]), (i, bd_ref[...])])

    o_ref[...] = (acc_ref[...] / l_ref[...]).T


def moba_prompt(h2, rel_bias, n_heads, q_col, k_col, v_col):
    L = h2.shape[0]
    nb = L // MOBA_BLOCK
    assert L % MOBA_BLOCK == 0
    nbp = -(-nb // SUBLANES) * SUBLANES
    T = MOBA_BLOCK
    kmean = pl.pallas_call(
        functools.partial(_kmean_kernel, nb=nb),
        grid=(n_heads,),
        in_specs=[pl.BlockSpec((L, HD_ATTN), lambda h: (0, k_col + h))],
        out_specs=pl.BlockSpec((None, nbp, HD_ATTN), lambda h: (h, 0, 0)),
        out_shape=jax.ShapeDtypeStruct((n_heads, nbp, HD_ATTN), F32),
        compiler_params=_params(("parallel",)),
    )(h2)

    ki = jnp.arange(T)[:, None]
    qi = jnp.arange(T)[None, :]
    bias_diag = _bias_of_dist(rel_bias, qi - ki)
    bias_prev = _bias_of_dist(rel_bias, qi - ki + T)
    bias_far = jnp.broadcast_to(rel_bias[NUM_BUCKETS - 1][:, None, None], (n_heads, 1, T))

    return pl.pallas_call(
        functools.partial(_flash_kernel, nb=nb),
        grid=(n_heads, nb),
        in_specs=[
            pl.BlockSpec((T, HD_ATTN), lambda h, i: (i, q_col + h)),
            pl.BlockSpec((L, HD_ATTN), lambda h, i: (0, k_col + h)),
            pl.BlockSpec((L, HD_ATTN), lambda h, i: (0, v_col + h)),
            pl.BlockSpec((None, nbp, HD_ATTN), lambda h, i: (h, 0, 0)),
            pl.BlockSpec((None, T, T), lambda h, i: (h, 0, 0)),
            pl.BlockSpec((None, T, T), lambda h, i: (h, 0, 0)),
            pl.BlockSpec((None, 1, T), lambda h, i: (h, 0, 0)),
        ],
        out_specs=pl.BlockSpec((T, HD_ATTN), lambda h, i: (i, h)),
        out_shape=jax.ShapeDtypeStruct((L, n_heads * HD_ATTN), F32),
        scratch_shapes=[
            pltpu.VMEM((nb, T, HD_ATTN), BF16),
            pltpu.VMEM((nb, HD_ATTN, T), BF16),
            pltpu.VMEM((T, HD_ATTN), BF16),
            pltpu.VMEM((nbp, T), F32),
            pltpu.VMEM((1, T), F32),
            pltpu.VMEM((1, T), F32),
            pltpu.VMEM((HD_ATTN, T), F32),
        ],
        compiler_params=_params(("parallel", "arbitrary")),
    )(h2, h2, h2, kmean, bias_diag, bias_prev, bias_far)


QPAD = 8
SAMPLE_BLOCKS_PER_STEP = 4


def _head_rows(ref, h, n_heads):
    return ref[pl.ds(h, ref.shape[0] // n_heads, stride=n_heads), :]


def _sample_p1_kernel(pt_ref, *refs, bps, n_heads):
    n = pl.program_id(1)
    page_refs = refs[:2 * bps]
    w_ref, lg_ref, gate_ref = refs[2 * bps:]
    w = w_ref[...].astype(BF16)
    P = page_refs[0].shape[0] // n_heads
    C = w.shape[1]
    col_head = lax.broadcasted_iota(jnp.int32, (P, C), 1) // QPAD
    for blk in range(bps):
        gsum = jnp.zeros((1, C), F32)
        for half in range(2):
            ref = page_refs[2 * blk + half]
            s = jnp.zeros((P, C), F32)
            for h in range(n_heads):
                sh = jnp.dot(_head_rows(ref, h, n_heads).astype(BF16), w, preferred_element_type=F32)
                s = jnp.where(col_head == h, sh, s)
            gsum = gsum + jnp.sum(s, axis=0, keepdims=True)
            lg_ref[blk, pl.ds(half * P, P), :] = s * (HD_ATTN ** -0.5)
        gate_ref[pl.ds(n * bps + blk, 1), :] = gsum * (1.0 / MOBA_BLOCK)


def _sample_p2_kernel(lg_ref, gate_ref, kn_ref, w_ref, bl_ref, bf_ref, bo_ref, p_ref, pn_ref, *, nbp, n_heads):
    gate = gate_ref[...]
    selm = _select_topk(gate, jnp.ones(gate.shape, jnp.bool_), 0)
    s = lg_ref[...] + selm[:, None, :]
    blk = lax.broadcasted_iota(jnp.int32, (nbp, 1, 1), 0)
    s = s + jnp.where(blk == nbp - 1, bl_ref[...][None], bf_ref[...][None])
    w = w_ref[...].astype(BF16)
    col_head = lax.broadcasted_iota(jnp.int32, (QPAD, w.shape[1]), 1) // QPAD
    sn = jnp.zeros((QPAD, w.shape[1]), F32)
    for h in range(n_heads):
        sn = jnp.where(col_head == h, jnp.dot(_head_rows(kn_ref, h, n_heads).astype(BF16), w,
                                              preferred_element_type=F32), sn)
    sn = sn * (HD_ATTN ** -0.5) + bo_ref[...]
    m = jnp.maximum(jnp.max(jnp.max(s, axis=1), axis=0, keepdims=True), jnp.max(sn, axis=0, keepdims=True))
    p = jnp.exp(s - m[None])
    pn = jnp.exp(sn - m)
    den = jnp.sum(jnp.sum(p, axis=1), axis=0, keepdims=True) + jnp.sum(pn, axis=0, keepdims=True)
    inv = 1.0 / den
    p = p * inv[None]
    for blk in range(nbp):
        p_ref[blk] = p[blk].T
    pn_ref[...] = pn * inv


def _sample_p3_kernel(pt_ref, *refs, n_heads, bps):
    n = pl.program_id(1)
    page_refs = refs[:2 * bps]
    p_ref, pn_ref, vn_ref, o_ref, acc_ref = refs[2 * bps:]
    P = page_refs[0].shape[0] // n_heads

    head_rows = lambda h: pl.ds(h * QPAD, QPAD)

    @pl.when(n == 0)
    def _():
        pn = pn_ref[...].astype(BF16)
        for h in range(n_heads):
            full = lax.dot_general(pn, _head_rows(vn_ref, h, n_heads).astype(BF16), (((0,), (0,)), ((), ())),
                                   preferred_element_type=F32)
            acc_ref[h] = full[h * QPAD:(h + 1) * QPAD]

    for h in range(n_heads):
        acc = acc_ref[h]
        for blk in range(bps):
            for half in range(2):
                p = p_ref[blk, head_rows(h), pl.ds(half * P, P)].astype(BF16)
                acc = acc + jnp.dot(p, _head_rows(page_refs[2 * blk + half], h, n_heads).astype(BF16),
                                    preferred_element_type=F32)
        acc_ref[h] = acc

    @pl.when(n == pl.num_programs(1) - 1)
    def _():
        for h in range(n_heads):
            o_ref[:, pl.ds(h * HD_ATTN, HD_ATTN)] = acc_ref[h]


def moba_sample(q, k_new, v_new, cache_k, cache_v, layer, page_table, rel_bias):
    B, L, H, HD = q.shape
    page = cache_k.shape[2]
    n_pages = page_table.shape[1]
    past = n_pages * page
    assert MOBA_BLOCK == 2 * page and past % MOBA_BLOCK == 0 and L <= QPAD
    nbp = past // MOBA_BLOCK
    C = H * QPAD
    CH = H * HD
    qp = jnp.pad(q, ((0, 0), (0, QPAD - L), (0, 0), (0, 0)))
    wq = qp.transpose(0, 3, 2, 1).reshape(B, HD, C)

    qpos = past + jnp.arange(QPAD)
    kl = (nbp - 1) * MOBA_BLOCK + jnp.arange(MOBA_BLOCK)
    bias_last = _bias_of_dist(rel_bias, qpos[None, :] - kl[:, None])
    bias_last = jnp.moveaxis(bias_last, 0, 1).reshape(MOBA_BLOCK, C)
    bias_far = jnp.repeat(rel_bias[NUM_BUCKETS - 1], QPAD).reshape(1, C)
    kown = past + jnp.arange(QPAD)
    dist_own = qpos[None, :] - kown[:, None]
    ok = (jnp.arange(QPAD)[:, None] < L) & (jnp.arange(QPAD)[None, :] < L)
    bias_own = _bias_of_dist(rel_bias, jnp.where(ok, dist_own, -1))
    bias_own = jnp.moveaxis(bias_own, 0, 1).reshape(QPAD, C)

    rows_ph = lambda t: t.reshape(t.shape[:-3] + (t.shape[-3] * H, HD))
    knp = rows_ph(jnp.pad(k_new, ((0, 0), (0, QPAD - L), (0, 0), (0, 0))))
    vnp = rows_ph(jnp.pad(v_new, ((0, 0), (0, QPAD - L), (0, 0), (0, 0))))
    cache_k, cache_v = rows_ph(cache_k), rows_ph(cache_v)

    bps = _pick(nbp, SAMPLE_BLOCKS_PER_STEP)
    page_specs = [pl.BlockSpec((None, None, page * H, HD),
                               functools.partial(lambda b, n, pt, j: (layer, pt[b, 2 * bps * n + j], 0, 0), j=j))
                  for j in range(2 * bps)]
    logits, gate = pl.pallas_call(
        functools.partial(_sample_p1_kernel, bps=bps, n_heads=H),
        grid_spec=pltpu.PrefetchScalarGridSpec(
            num_scalar_prefetch=1,
            grid=(B, nbp // bps),
            in_specs=page_specs + [pl.BlockSpec((None, HD, C), lambda b, n, pt: (b, 0, 0))],
            out_specs=[pl.BlockSpec((None, bps, MOBA_BLOCK, C), lambda b, n, pt: (b, n, 0, 0)),
                       pl.BlockSpec((None, nbp, C), lambda b, n, pt: (b, 0, 0))],
        ),
        out_shape=[jax.ShapeDtypeStruct((B, nbp, MOBA_BLOCK, C), F32),
                   jax.ShapeDtypeStruct((B, nbp, C), F32)],
        compiler_params=_params(("parallel", "arbitrary")),
    )(page_table, *([cache_k] * (2 * bps)), wq)

    p_past, p_new = pl.pallas_call(
        functools.partial(_sample_p2_kernel, nbp=nbp, n_heads=H),
        grid=(B,),
        in_specs=[
            pl.BlockSpec((None, nbp, MOBA_BLOCK, C), lambda b: (b, 0, 0, 0)),
            pl.BlockSpec((None, nbp, C), lambda b: (b, 0, 0)),
            pl.BlockSpec((None, QPAD * H, HD), lambda b: (b, 0, 0)),
            pl.BlockSpec((None, HD, C), lambda b: (b, 0, 0)),
            pl.BlockSpec((MOBA_BLOCK, C), lambda b: (0, 0)),
            pl.BlockSpec((1, C), lambda b: (0, 0)),
            pl.BlockSpec((QPAD, C), lambda b: (0, 0)),
        ],
        out_specs=[pl.BlockSpec((None, nbp, C, MOBA_BLOCK), lambda b: (b, 0, 0, 0)),
                   pl.BlockSpec((None, QPAD, C), lambda b: (b, 0, 0))],
        out_shape=[jax.ShapeDtypeStruct((B, nbp, C, MOBA_BLOCK), F32),
                   jax.ShapeDtypeStruct((B, QPAD, C), F32)],
        compiler_params=_params(("parallel",)),
    )(logits, gate, knp, wq, bias_last, bias_far, bias_own)

    o = pl.pallas_call(
        functools.partial(_sample_p3_kernel, n_heads=H, bps=bps),
        grid_spec=pltpu.PrefetchScalarGridSpec(
            num_scalar_prefetch=1,
            grid=(B, nbp // bps),
            in_specs=page_specs + [
                pl.BlockSpec((None, bps, C, MOBA_BLOCK), lambda b, n, pt: (b, n, 0, 0)),
                pl.BlockSpec((None, QPAD, C), lambda b, n, pt: (b, 0, 0)),
                pl.BlockSpec((None, QPAD * H, HD), lambda b, n, pt: (b, 0, 0))],
            out_specs=pl.BlockSpec((None, QPAD, CH), lambda b, n, pt: (b, 0, 0)),
            scratch_shapes=[pltpu.VMEM((H, QPAD, HD), F32)],
        ),
        out_shape=jax.ShapeDtypeStruct((B, QPAD, CH), F32),
        compiler_params=_params(("parallel", "arbitrary")),
    )(page_table, *([cache_v] * (2 * bps)), p_past, p_new, vnp)
    return o[:, :L]


def _head_minor(x, n_heads, axis=-1):
    axis = axis % x.ndim
    shp = x.shape
    x = x.reshape(shp[:axis] + (n_heads, shp[axis] // n_heads) + shp[axis + 1:])
    return jnp.swapaxes(x, axis, axis + 1).reshape(shp)


def _head_sum(x, n_heads):
    nt = x.shape[1] // LANES
    t = x[:, 0:LANES]
    for c in range(1, nt):
        t = t + x[:, c * LANES:(c + 1) * LANES]
    shift = n_heads
    while shift < LANES:
        t = t + pltpu.roll(t, shift, axis=1)
        shift *= 2
    return jnp.concatenate([t] * nt, axis=1)


def _rwkv_prep_kernel(k_ref, a_ref, kk_ref, ka_ref, k2_ref, na_ref, nb_ref, *, n_heads):
    k = k_ref[...]
    a = a_ref[...]
    kk = k * kk_ref[...]
    nrm = jnp.maximum(jnp.sqrt(_head_sum(kk * kk, n_heads)), 1e-12)
    kk = kk / nrm
    k2_ref[...] = k * (1.0 + (a - 1.0) * ka_ref[...])
    na_ref[...] = -kk
    nb_ref[...] = kk * a


def rwkv_prep(k, a, k_k, k_a, n_heads, *, tm=256):
    M, D = k.shape
    tm = _pick(M, tm)
    blk = pl.BlockSpec((tm, D), lambda i: (i, 0))
    row = pl.BlockSpec((1, D), lambda i: (0, 0))
    return pl.pallas_call(
        functools.partial(_rwkv_prep_kernel, n_heads=n_heads),
        grid=(M // tm,),
        in_specs=[blk, blk, row, row],
        out_specs=[blk, blk, blk],
        out_shape=[jax.ShapeDtypeStruct((M, D), F32)] * 3,
        compiler_params=_params(("parallel",)),
    )(k, a, k_k.reshape(1, D), k_a.reshape(1, D))


def _rwkv_post_kernel(y_ref, r_ref, k2_ref, v_ref, g_ref, rk_ref, lg_ref, lb_ref, o_ref, *, n_heads):
    y = y_ref[...]
    inv = 1.0 / HS_RWKV
    mu = _head_sum(y, n_heads) * inv
    yc = y - mu
    var = _head_sum(yc * yc, n_heads) * inv
    yn = yc * lax.rsqrt(var + GN_EPS) * lg_ref[...] + lb_ref[...]
    bonus = _head_sum(r_ref[...] * k2_ref[...] * rk_ref[...], n_heads) * v_ref[...]
    o_ref[...] = (yn + bonus) * g_ref[...]


def rwkv_post(y, r, k2, v, g, r_k, lnx_g, lnx_b, n_heads, *, tm=256):
    M, D = y.shape
    tm = _pick(M, tm)
    blk = pl.BlockSpec((tm, D), lambda i: (i, 0))
    row = pl.BlockSpec((1, D), lambda i: (0, 0))
    return pl.pallas_call(
        functools.partial(_rwkv_post_kernel, n_heads=n_heads),
        grid=(M // tm,),
        in_specs=[blk] * 5 + [row] * 3,
        out_specs=blk,
        out_shape=jax.ShapeDtypeStruct((M, D), F32),
        compiler_params=_params(("parallel",)),
    )(y, r, k2, v, g, r_k.reshape(1, D), lnx_g.reshape(1, D), lnx_b.reshape(1, D))


def _split3(x):
    hi = x.astype(BF16)
    r1 = x - hi.astype(F32)
    mid = r1.astype(BF16)
    lo = (r1 - mid.astype(F32)).astype(BF16)
    return hi, mid, lo


SCAN_GROUP = 16


def _scan_kernel(r_ref, d_ref, k_ref, a_ref, b_ref, v_ref, s0_ref, y_ref, so_ref, cola_ref, colb_ref, s_ref, vy_ref,
                 *, ng, nh, nvh, n_valid):
    c = pl.program_id(1)
    G = SCAN_GROUP
    klo = LANES // nh
    nt = HS_RWKV // klo

    @pl.when(c == 0)
    def _():
        s_ref[...] = s0_ref[...]

    rr = lax.broadcasted_iota(jnp.int32, (LANES, klo * LANES), 0)
    cc = lax.broadcasted_iota(jnp.int32, (LANES, klo * LANES), 1)
    spread = ((rr // nh == cc // LANES) & (rr % nh == cc % nh)).astype(BF16)
    spread3 = jnp.concatenate([spread, spread, spread], axis=0)
    ops = (r_ref, d_ref, k_ref, a_ref, b_ref)

    def build(gi, col_ref):
        rows = pl.ds(pl.multiple_of(gi * G, G), G)
        pieces = []
        for ref in ops:
            hi, mid, lo = _split3(ref[rows, :])
            for jt in range(nt):
                ls = slice(jt * LANES, (jt + 1) * LANES)
                pieces.append(jnp.concatenate([hi[:, ls], mid[:, ls], lo[:, ls]], axis=1))
        out = jnp.dot(jnp.concatenate(pieces, axis=0), spread3, preferred_element_type=F32)
        for o in range(len(ops)):
            for jt in range(nt):
                src = (o * nt + jt) * G
                for kl in range(klo):
                    dst = (o * HS_RWKV + jt * klo + kl) * G
                    col_ref[pl.ds(dst, G), :] = out[src:src + G, kl * LANES:(kl + 1) * LANES]

    def consume(gi, col_ref, ntok):
        rows = pl.ds(pl.multiple_of(gi * G, G), G)
        vy_ref[...] = v_ref[rows, :]
        for tt in range(ntok):
            tile = lambda o: col_ref[pl.ds(o * HS_RWKV * G + tt, HS_RWKV, stride=G), :]
            r_c, d_c, k_c, a_c, b_c = (tile(o) for o in range(5))
            for vh in range(nvh):
                lanes = pl.ds(vh * LANES, LANES)
                S = s_ref[vh]
                sa = jnp.sum(S * a_c, axis=0, keepdims=True)
                S = S * d_c + sa * b_c + vy_ref[pl.ds(tt, 1), lanes] * k_c
                s_ref[vh] = S
                vy_ref[pl.ds(tt, 1), lanes] = jnp.sum(S * r_c, axis=0, keepdims=True)
        y_ref[rows, :] = vy_ref[...]

    build(0, cola_ref)
    if ng % 2:
        assert ng == 1
        consume(0, cola_ref, n_valid)
    else:
        assert n_valid == G
        def pair(gp, carry):
            g0 = 2 * gp
            build(g0 + 1, colb_ref)
            consume(g0, cola_ref, G)
            build(jnp.minimum(g0 + 2, ng - 1), cola_ref)
            consume(g0 + 1, colb_ref, G)
            return carry

        lax.fori_loop(0, ng // 2, pair, 0)

    @pl.when(c == pl.num_programs(1) - 1)
    def _():
        so_ref[...] = s_ref[...]


def rwkv_scan(r, d, k, a, b, v, state, n_heads, *, tc=64):
    B, L, D = r.shape
    H = n_heads
    G = SCAN_GROUP
    vlo = LANES // H
    nvh = HS_RWKV // vlo
    assert H * vlo == LANES and nvh * vlo == HS_RWKV
    Lp = -(-L // G) * G
    n_valid = G
    if Lp != L:
        assert Lp == G
        n_valid = L
        r, d, k, a, b, v = (jnp.pad(t, ((0, 0), (0, Lp - L), (0, 0))) for t in (r, d, k, a, b, v))
    tc = _pick(Lp, tc)
    ng = tc // G
    s0 = state.reshape(B, H, nvh, vlo, HS_RWKV).transpose(0, 2, 4, 3, 1).reshape(B, nvh, HS_RWKV, LANES)
    xspec = pl.BlockSpec((None, tc, D), lambda bb, c: (bb, c, 0))
    sspec = pl.BlockSpec((None, nvh, HS_RWKV, LANES), lambda bb, c: (bb, 0, 0, 0))
    y, s1 = pl.pallas_call(
        functools.partial(_scan_kernel, ng=ng, nh=H, nvh=nvh, n_valid=n_valid),
        grid=(B, Lp // tc),
        in_specs=[xspec] * 6 + [sspec],
        out_specs=[xspec, sspec],
        out_shape=[jax.ShapeDtypeStruct((B, Lp, D), F32),
                   jax.ShapeDtypeStruct((B, nvh, HS_RWKV, LANES), F32)],
        scratch_shapes=[pltpu.VMEM((5 * HS_RWKV * G, LANES), F32),
                        pltpu.VMEM((5 * HS_RWKV * G, LANES), F32),
                        pltpu.VMEM((nvh, HS_RWKV, LANES), F32),
                        pltpu.VMEM((G, D), F32)],
        compiler_params=_params(("parallel", "arbitrary")),
    )(r, d, k, a, b, v, s0)
    s1 = s1.reshape(B, nvh, HS_RWKV, vlo, H).transpose(0, 4, 1, 3, 2).reshape(B, H, HS_RWKV, HS_RWKV)
    return y[:, :L], s1


def _conv_attn_layer(x, conv_prev, attn_fn, w_in, conv_w, conv_b, cln_g, cln_b, w_out, ln_g, ln_b, alpha,
                     c_conv, c_attn):
    B, L, D = x.shape
    x2 = x.reshape(B * L, D)
    h = matmul(x2, w_in, tm=1024).reshape(B, L, -1)
    c, cstate = conv_module(h, conv_prev, conv_w, conv_b, cln_g, cln_b, c_conv)
    o = attn_fn(h)
    co = jnp.concatenate([c, o], axis=-1).reshape(B * L, c_conv + c_attn)
    xn = rowmm_ln(co, w_out, x2, ln_g, ln_b, alpha).reshape(B, L, D)
    k = h[..., 2 * c_conv + c_attn:2 * c_conv + 2 * c_attn]
    v = h[..., 2 * c_conv + 2 * c_attn:]
    return xn, k, v, cstate


def _rwkv_layer(x, shift_prev, wkv_prev, p, ln_g, ln_b, alpha, n_heads):
    B, L, D = x.shape
    x_prev = jnp.concatenate([shift_prev[:, None], x[:, :-1]], axis=1).reshape(B * L, D)
    x2 = x.reshape(B * L, D)
    mixrow = lambda i: p["mix"][i].reshape(1, D)
    proj = lambda i, w, **kw: matmul(x2, w, xprev=x_prev, mixrow=mixrow(i), tn=2048, **kw)
    r = proj(0, p["wr"])
    lw = proj(1, p["w1"], epilogue="tanh")
    d = matmul(lw, p["w2"], bias=p["w0"].reshape(1, D), epilogue="decay", tn=2048)
    k = proj(2, p["wk"])
    v = proj(3, p["wv"])
    la = proj(4, p["a1"])
    a = matmul(la, p["a2"], bias=p["a0"].reshape(1, D), epilogue="sigmoid", tn=2048)
    lg = proj(5, p["g1"], epilogue="sigmoid")
    g = matmul(lg, p["g2"], tn=2048)
    k2, na, nb = rwkv_prep(k, a, p["kk"], p["ka"], n_heads)
    sh = lambda t: t.reshape(B, L, D)
    y, s1 = rwkv_scan(sh(r), sh(d), sh(k2), sh(na), sh(nb), sh(v), wkv_prev, n_heads)
    yg = rwkv_post(y.reshape(B * L, D), r, k2, v, g, p["rk"], p["lnx_g"], p["lnx_b"], n_heads)
    xn = rowmm_ln(yg, p["wo"], x2, ln_g, ln_b, alpha).reshape(B, L, D)
    return xn, s1, x[:, -1]


def kernel(x_prompt, x_sample, cache_k, cache_v, page_table, state_conv, state_wkv, state_shift, rel_bias, ab_w_in, ab_conv_w, ab_conv_b, ab_ln_g, ab_ln_b, ab_w_out, rw_mix, rw_wr, rw_wk, rw_wv, rw_wo, rw_w0, rw_w1, rw_w2, rw_a0, rw_a1, rw_a2, rw_g1, rw_g2, rw_kk, rw_ka, rw_rk, rw_lnx_g, rw_lnx_b, ln_g, ln_b, mlp_w1, mlp_w2):
    B, L, D = x_prompt.shape
    DB, DL, _ = x_sample.shape
    depth = ln_g.shape[0]
    alpha = (2 * depth) ** 0.25
    c_conv = ab_conv_w.shape[2]
    n_heads_attn = cache_k.shape[3]
    c_attn = n_heads_attn * HD_ATTN
    n_heads_rwkv = D // HS_RWKV
    assert B == 1
    bf = lambda w: w.astype(BF16)
    row = lambda a: a.reshape(1, -1)
    q_col = 2 * c_conv // LANES
    k_col = q_col + c_attn // LANES
    v_col = k_col + c_attn // LANES

    xp, xs = x_prompt, x_sample
    outs = {n: [] for n in ("kp", "vp", "ks", "vs", "cp", "cs", "wp", "ws", "hp", "hs")}
    for layer in range(depth):
        i = layer // 2
        g0, b0, g1, b1 = row(ln_g[layer, 0]), row(ln_b[layer, 0]), row(ln_g[layer, 1]), row(ln_b[layer, 1])
        if layer % 2 == 0:
            ab = (bf(ab_w_in[i]), ab_conv_w[i], ab_conv_b[i], ab_ln_g[i], ab_ln_b[i], bf(ab_w_out[i]), g0, b0, alpha,
                  c_conv, c_attn)
            attn_p = lambda h: moba_prompt(h.reshape(L, -1), rel_bias, n_heads_attn, q_col, k_col, v_col)[None]
            xp, kp, vp, cp = _conv_attn_layer(xp, jnp.zeros((B, CONV_W - 1, c_conv), F32), attn_p, *ab)

            def attn_s(h):
                hd = lambda lo: h[..., lo:lo + c_attn].reshape(DB, DL, n_heads_attn, HD_ATTN)
                return moba_sample(hd(2 * c_conv), hd(2 * c_conv + c_attn), hd(2 * c_conv + 2 * c_attn),
                                   cache_k, cache_v, i, page_table, rel_bias)

            xs, ks, vs, cs = _conv_attn_layer(xs, state_conv[i], attn_s, *ab)
            hd4 = lambda t: t.reshape(t.shape[0], t.shape[1], n_heads_attn, HD_ATTN)
            outs["kp"].append(hd4(kp)); outs["vp"].append(hd4(vp))
            outs["ks"].append(hd4(ks)); outs["vs"].append(hd4(vs))
            outs["cp"].append(cp); outs["cs"].append(cs)
        else:
            hm = functools.partial(_head_minor, n_heads=n_heads_rwkv)
            p = dict(mix=rw_mix[i], wr=bf(hm(rw_wr[i])), wk=bf(hm(rw_wk[i])), wv=bf(hm(rw_wv[i])),
                     wo=bf(hm(rw_wo[i], axis=0)), w0=hm(rw_w0[i]), w1=bf(rw_w1[i]), w2=bf(hm(rw_w2[i])),
                     a0=hm(rw_a0[i]), a1=bf(rw_a1[i]), a2=bf(hm(rw_a2[i])), g1=bf(rw_g1[i]), g2=bf(hm(rw_g2[i])),
                     kk=hm(rw_kk[i]), ka=hm(rw_ka[i]), rk=hm(rw_rk[i].reshape(-1)),
                     lnx_g=hm(rw_lnx_g[i]), lnx_b=hm(rw_lnx_b[i]))
            xp, wp, hp = _rwkv_layer(xp, jnp.zeros((B, D), F32),
                                     jnp.zeros((B, n_heads_rwkv, HS_RWKV, HS_RWKV), F32), p, g0, b0, alpha,
                                     n_heads_rwkv)
            xs, ws, hs = _rwkv_layer(xs, state_shift[i], state_wkv[i], p, g0, b0, alpha, n_heads_rwkv)
            outs["wp"].append(wp); outs["ws"].append(ws); outs["hp"].append(hp); outs["hs"].append(hs)
        w1b, w2b = bf(mlp_w1[layer]), bf(mlp_w2[layer])
        xp = mlp_ln(xp.reshape(B * L, D), w1b, w2b, g1, b1, alpha).reshape(B, L, D)
        xs = mlp_ln(xs.reshape(DB * DL, D), w1b, w2b, g1, b1, alpha).reshape(DB, DL, D)
    st = lambda n: jnp.stack(outs[n])
    return (xp, xs, st("kp"), st("vp"), st("ks"), st("vs"), st("cp"), st("cs"),
            st("wp"), st("ws"), st("hp"), st("hs"))
```

```python
import functools
import math

import jax
import jax.numpy as jnp
import numpy as np
from jax import lax
from jax.experimental import pallas as pl
from jax.experimental.pallas import tpu as pltpu

F32 = jnp.float32
BF16 = jnp.bfloat16
HIGHEST = lax.Precision.HIGHEST

LANES = 128
SUBLANES = 8
VMEM_LIMIT = 56 * 1024 * 1024

CONV_W = 31
HD_ATTN = 128
MOBA_BLOCK = 256
MOBA_TOPK = 3
NUM_BUCKETS = 32
MAX_DISTANCE = 128
HS_RWKV = 64
GN_EPS = 64e-5
LN_EPS = 1e-5
NEG = -1e30


def _params(sem):
    return pltpu.CompilerParams(dimension_semantics=sem, vmem_limit_bytes=VMEM_LIMIT)


def _pick(n, pref):
    if n <= pref:
        return n
    t = pref
    while n % t:
        t //= 2
    return t


def _softplus(y):
    return jnp.maximum(y, 0.0) + jnp.log(1.0 + jnp.exp(-jnp.abs(y)))


def _sigmoid(y):
    return 1.0 / (1.0 + jnp.exp(-y))


def _epilogue(acc, kind):
    if kind == "none":
        return acc
    if kind == "tanh":
        return jnp.tanh(acc)
    if kind == "sigmoid":
        return _sigmoid(acc)
    if kind == "decay":
        return jnp.exp(-jnp.exp(-_softplus(-acc) - 0.5))
    raise ValueError(kind)


def _mm_kernel(*refs, mix, bias, epilogue):
    it = iter(refs)
    x_ref = next(it)
    xp_ref = next(it) if mix else None
    m_ref = next(it) if mix else None
    w_ref = next(it)
    b_ref = next(it) if bias else None
    o_ref = next(it)
    xs_ref = next(it)

    @pl.when(pl.program_id(1) == 0)
    def _():
        x = x_ref[...]
        if mix:
            x = x + (xp_ref[...] - x) * m_ref[...]
        xs_ref[...] = x.astype(BF16)

    acc = jnp.dot(xs_ref[...], w_ref[...], preferred_element_type=F32)
    if bias:
        acc = acc + b_ref[...]
    o_ref[...] = _epilogue(acc, epilogue)


def matmul(x, w, *, xprev=None, mixrow=None, bias=None, epilogue="none", tm=512, tn=512):
    M, K = x.shape
    N = w.shape[1]
    tm = _pick(M, tm)
    tn = _pick(N, tn)
    mix = xprev is not None
    ins = [x]
    specs = [pl.BlockSpec((tm, K), lambda i, j: (i, 0))]
    if mix:
        ins += [xprev, mixrow]
        specs += [pl.BlockSpec((tm, K), lambda i, j: (i, 0)), pl.BlockSpec((1, K), lambda i, j: (0, 0))]
    ins.append(w)
    specs.append(pl.BlockSpec((K, tn), lambda i, j: (0, j)))
    if bias is not None:
        ins.append(bias)
        specs.append(pl.BlockSpec((1, tn), lambda i, j: (0, j)))
    return pl.pallas_call(
        functools.partial(_mm_kernel, mix=mix, bias=bias is not None, epilogue=epilogue),
        grid=(M // tm, N // tn),
        in_specs=specs,
        out_specs=pl.BlockSpec((tm, tn), lambda i, j: (i, j)),
        out_shape=jax.ShapeDtypeStruct((M, N), F32),
        scratch_shapes=[pltpu.VMEM((tm, K), BF16)],
        compiler_params=_params(("parallel", "arbitrary")),
    )(*ins)


def _layer_norm(z, g, b):
    mu = jnp.mean(z, axis=-1, keepdims=True)
    zc = z - mu
    var = jnp.mean(zc * zc, axis=-1, keepdims=True)
    return zc * lax.rsqrt(var + LN_EPS) * g + b


def _rowmm_ln_kernel(x_ref, w_ref, r_ref, g_ref, b_ref, o_ref, acc_ref, *, alpha):
    k = pl.program_id(1)

    @pl.when(k == 0)
    def _():
        acc_ref[...] = jnp.zeros_like(acc_ref)

    acc_ref[...] += jnp.dot(x_ref[...].astype(BF16), w_ref[...], preferred_element_type=F32)

    @pl.when(k == pl.num_programs(1) - 1)
    def _():
        o_ref[...] = _layer_norm(alpha * r_ref[...] + acc_ref[...], g_ref[...], b_ref[...])


def rowmm_ln(x, w, resid, g, b, alpha, *, tm=512, tk=2048):
    M, K = x.shape
    N = w.shape[1]
    tm = _pick(M, tm)
    tk = _pick(K, tk)
    return pl.pallas_call(
        functools.partial(_rowmm_ln_kernel, alpha=alpha),
        grid=(M // tm, K // tk),
        in_specs=[
            pl.BlockSpec((tm, tk), lambda i, k: (i, k)),
            pl.BlockSpec((tk, N), lambda i, k: (k, 0)),
            pl.BlockSpec((tm, N), lambda i, k: (i, 0)),
            pl.BlockSpec((1, N), lambda i, k: (0, 0)),
            pl.BlockSpec((1, N), lambda i, k: (0, 0)),
        ],
        out_specs=pl.BlockSpec((tm, N), lambda i, k: (i, 0)),
        out_shape=jax.ShapeDtypeStruct((M, N), F32),
        scratch_shapes=[pltpu.VMEM((tm, N), F32)],
        compiler_params=_params(("parallel", "arbitrary")),
    )(x, w, resid, g, b)


def _mlp_ln_kernel(x_ref, w1_ref, w2_ref, g_ref, b_ref, o_ref, xs_ref, acc_ref, *, alpha):
    f = pl.program_id(1)

    @pl.when(f == 0)
    def _():
        xs_ref[...] = x_ref[...].astype(BF16)
        acc_ref[...] = jnp.zeros_like(acc_ref)

    h = jnp.dot(xs_ref[...], w1_ref[...], preferred_element_type=F32)
    h = jnp.maximum(h, 0.0)
    h = (h * h).astype(BF16)
    acc_ref[...] += jnp.dot(h, w2_ref[...], preferred_element_type=F32)

    @pl.when(f == pl.num_programs(1) - 1)
    def _():
        o_ref[...] = _layer_norm(alpha * x_ref[...] + acc_ref[...], g_ref[...], b_ref[...])


def mlp_ln(x, w1, w2, g, b, alpha, *, tm=1024, tf=512):
    M, D = x.shape
    FF = w1.shape[1]
    tm = _pick(M, tm)
    tf = _pick(FF, tf)
    once = pl.Buffered(1)
    return pl.pallas_call(
        functools.partial(_mlp_ln_kernel, alpha=alpha),
        grid=(M // tm, FF // tf),
        in_specs=[
            pl.BlockSpec((tm, D), lambda i, f: (i, 0), pipeline_mode=once),
            pl.BlockSpec((D, tf), lambda i, f: (0, f)),
            pl.BlockSpec((tf, D), lambda i, f: (f, 0)),
            pl.BlockSpec((1, D), lambda i, f: (0, 0)),
            pl.BlockSpec((1, D), lambda i, f: (0, 0)),
        ],
        out_specs=pl.BlockSpec((tm, D), lambda i, f: (i, 0)),
        out_shape=jax.ShapeDtypeStruct((M, D), F32),
        scratch_shapes=[pltpu.VMEM((tm, D), BF16), pltpu.VMEM((tm, D), F32)],
        compiler_params=_params(("parallel", "arbitrary")),
    )(x, w1, w2, g, b)


HALO = 32


def _conv_kernel(hv_ref, hg_ref, prev_ref, w_ref, cb_ref, g_ref, b_ref, c_ref, st_ref, ue_ref, win_ref, *, tl):
    l = pl.program_id(1)
    npv = CONV_W - 1
    off = HALO - npv

    @pl.when(l == 0)
    def _():
        ue_ref[pl.ds(off, npv), :] = prev_ref[...]

    u = hv_ref[...] * _sigmoid(hg_ref[...])
    ue_ref[pl.ds(HALO, tl), :] = u
    acc = jnp.zeros_like(u) + cb_ref[...]
    if tl % SUBLANES == 0:
        for r in range(min(SUBLANES, CONV_W)):
            span = (CONV_W - 1 - r) // SUBLANES * SUBLANES
            win_ref[pl.ds(0, tl + span), :] = ue_ref[pl.ds(off + r, tl + span), :]
            for j in range(r, CONV_W, SUBLANES):
                acc = acc + win_ref[pl.ds(j - r, tl), :] * w_ref[pl.ds(j, 1), :]
    else:
        for j in range(CONV_W):
            acc = acc + ue_ref[pl.ds(off + j, tl), :] * w_ref[pl.ds(j, 1), :]
    y = _layer_norm(acc, g_ref[...], b_ref[...])
    c_ref[...] = y * _sigmoid(y)
    tail = ue_ref[pl.ds(off + tl, npv), :]
    ue_ref[pl.ds(off, npv), :] = tail

    @pl.when(l == pl.num_programs(1) - 1)
    def _():
        st_ref[...] = tail


def conv_module(h3, conv_prev, conv_w, conv_b, ln_g, ln_b, c_conv, *, tl=256):
    B, L, _ = h3.shape
    tl = _pick(L, tl)
    npv = CONV_W - 1
    row = lambda a: a.reshape(1, c_conv)
    return pl.pallas_call(
        functools.partial(_conv_kernel, tl=tl),
        grid=(B, L // tl),
        in_specs=[
            pl.BlockSpec((None, tl, c_conv), lambda b, l: (b, l, 0)),
            pl.BlockSpec((None, tl, c_conv), lambda b, l: (b, l, 1)),
            pl.BlockSpec((None, npv, c_conv), lambda b, l: (b, 0, 0)),
            pl.BlockSpec((CONV_W, c_conv), lambda b, l: (0, 0)),
            pl.BlockSpec((1, c_conv), lambda b, l: (0, 0)),
            pl.BlockSpec((1, c_conv), lambda b, l: (0, 0)),
            pl.BlockSpec((1, c_conv), lambda b, l: (0, 0)),
        ],
        out_specs=[
            pl.BlockSpec((None, tl, c_conv), lambda b, l: (b, l, 0)),
            pl.BlockSpec((None, npv, c_conv), lambda b, l: (b, 0, 0)),
        ],
        out_shape=[
            jax.ShapeDtypeStruct((B, L, c_conv), F32),
            jax.ShapeDtypeStruct((B, npv, c_conv), F32),
        ],
        scratch_shapes=[pltpu.VMEM((HALO + tl, c_conv), F32), pltpu.VMEM((HALO + tl, c_conv), F32)],
        compiler_params=_params(("arbitrary", "arbitrary")),
    )(h3, h3, conv_prev, conv_w, row(conv_b), row(ln_g), row(ln_b))


def _t5_bucket(dist):
    n = jnp.maximum(dist, 0)
    max_exact = NUM_BUCKETS // 2
    nf = jnp.maximum(n, 1).astype(F32)
    large = max_exact + (jnp.log(nf / max_exact) / math.log(MAX_DISTANCE / max_exact)
                         * (NUM_BUCKETS - max_exact)).astype(jnp.int32)
    large = jnp.minimum(large, NUM_BUCKETS - 1)
    return jnp.where(n < max_exact, n, large)


def _bias_of_dist(rel_bias, dist):
    onehot = (_t5_bucket(dist)[..., None] == jnp.arange(NUM_BUCKETS)).astype(F32)
    b = jnp.moveaxis(jnp.dot(onehot, rel_bias, precision=HIGHEST), -1, 0)
    return jnp.where(dist[None] >= 0, b, NEG)


def _kmean_kernel(k_ref, o_ref, *, nb):
    k = k_ref[...]
    km = jnp.mean(k.reshape(nb, MOBA_BLOCK, HD_ATTN), axis=1)
    o_ref[...] = jnp.zeros_like(o_ref)
    o_ref[pl.ds(0, nb), :] = km


def _select_topk(gate, valid, axis):
    idx = lax.broadcasted_iota(jnp.int32, gate.shape, axis).astype(F32)
    big = 1e9
    g = jnp.where(valid, gate, -jnp.inf)
    sel = jnp.zeros(gate.shape, jnp.bool_)
    for _ in range(MOBA_TOPK):
        mx = jnp.max(g, axis=axis, keepdims=True)
        is_max = (g == mx) & (mx > -jnp.inf)
        first = jnp.min(jnp.where(is_max, idx, big), axis=axis, keepdims=True)
        pick = idx == first
        sel = sel | pick
        g = jnp.where(pick, -jnp.inf, g)
    return jnp.where(sel, 0.0, NEG).astype(F32)


FAR_GROUP = 4


def _flash_kernel(q_ref, k_ref, v_ref, km_ref, bd_ref, bp_ref, bf_ref, o_ref,
                  kb_ref, vt_ref, qs_ref, sel_ref, m_ref, l_ref, acc_ref, sa_ref, sb_ref, *, nb):
    i = pl.program_id(1)
    T = MOBA_BLOCK

    @pl.when(i == 0)
    def _():
        def convert(n, carry):
            r0 = pl.multiple_of(n * T, T)
            kb_ref[n] = k_ref[pl.ds(r0, T), :].astype(BF16)
            vt_ref[n] = v_ref[pl.ds(r0, T), :].T.astype(BF16)
            return carry

        lax.fori_loop(0, nb, convert, 0)

    q = q_ref[...]
    qs_ref[...] = (q * (HD_ATTN ** -0.5)).astype(BF16)
    gate_t = lax.dot_general(km_ref[...], q, (((1,), (1,)), ((), ())),
                             precision=HIGHEST, preferred_element_type=F32)
    blk = lax.broadcasted_iota(jnp.int32, gate_t.shape, 0)
    sel_ref[...] = _select_topk(gate_t, blk < i, 0)
    m_ref[...] = jnp.full_like(m_ref, NEG)
    l_ref[...] = jnp.zeros_like(l_ref)
    acc_ref[...] = jnp.zeros_like(acc_ref)

    def raw_scores(j):
        return lax.dot_general(kb_ref[j], qs_ref[...], (((1,), (1,)), ((), ())), preferred_element_type=F32)

    def absorb(js, ss):
        m_old = m_ref[...]
        m_new = m_old
        for s in ss:
            m_new = jnp.maximum(m_new, jnp.max(s, axis=0, keepdims=True))
        alpha = jnp.exp(m_old - m_new)
        l_new = alpha * l_ref[...]
        acc = alpha * acc_ref[...]
        for j, s in zip(js, ss):
            p = jnp.exp(s - m_new)
            l_new = l_new + jnp.sum(p, axis=0, keepdims=True)
            acc = acc + jnp.dot(vt_ref[j], p.astype(BF16), preferred_element_type=F32)
        l_ref[...] = l_new
        acc_ref[...] = acc
        m_ref[...] = m_new

    def update(blocks):
        absorb([j for j, _ in blocks], [raw_scores(j) + extra for j, extra in blocks])

    def far_term(j):
        return bf_ref[...] + sel_ref[pl.ds(j, 1), :]

    n_far = jnp.maximum(i - 1, 0)
    n_groups = n_far // FAR_GROUP

    def fill(g, s_ref):
        for b in range(FAR_GROUP):
            s_ref[b] = raw_scores(g * FAR_GROUP + b)

    def drain(g, s_ref):
        js = [g * FAR_GROUP + b for b in range(FAR_GROUP)]
        absorb(js, [s_ref[b] + far_term(j) for b, j in enumerate(js)])

    @pl.when(n_groups > 0)
    def _():
        fill(0, sa_ref)

    def far_pair(gp, carry):
        g0 = 2 * gp
        fill(jnp.minimum(g0 + 1, n_groups - 1), sb_ref)
        drain(g0, sa_ref)
        fill(jnp.minimum(g0 + 2, n_groups - 1), sa_ref)
        drain(g0 + 1, sb_ref)
        return carry

    lax.fori_loop(0, n_groups // 2, far_pair, 0)

    @pl.when(n_groups % 2 == 1)
    def _():
        drain(n_groups - 1, sa_ref)

    def far_single(j, carry):
        update([(j, far_term(j))])
        return carry

    lax.fori_loop(n_groups * FAR_GROUP, n_far, far_single, 0)

    @pl.when(i == 0)
    def _():
        update([(i, bd_ref[...])])

    @pl.when(i >= 1)
    def _():
        update([(i - 1, bp_ref[...] + sel_ref[pl.ds(i - 1, 1), :]), (i, bd_ref[...])])

    o_ref[...] = (acc_ref[...] / l_ref[...]).T


def moba_prompt(h2, k2, v2, rel_bias, n_heads, q_col):
    L = h2.shape[0]
    nb = L // MOBA_BLOCK
    assert L % MOBA_BLOCK == 0
    nbp = -(-nb // SUBLANES) * SUBLANES
    T = MOBA_BLOCK
    kmean = pl.pallas_call(
        functools.partial(_kmean_kernel, nb=nb),
        grid=(n_heads,),
        in_specs=[pl.BlockSpec((L, HD_ATTN), lambda h: (0, h))],
        out_specs=pl.BlockSpec((None, nbp, HD_ATTN), lambda h: (h, 0, 0)),
        out_shape=jax.ShapeDtypeStruct((n_heads, nbp, HD_ATTN), F32),
        compiler_params=_params(("parallel",)),
    )(k2)

    ki = jnp.arange(T)[:, None]
    qi = jnp.arange(T)[None, :]
    bias_diag = _bias_of_dist(rel_bias, qi - ki)
    bias_prev = _bias_of_dist(rel_bias, qi - ki + T)
    bias_far = jnp.broadcast_to(rel_bias[NUM_BUCKETS - 1][:, None, None], (n_heads, 1, T))

    return pl.pallas_call(
        functools.partial(_flash_kernel, nb=nb),
        grid=(n_heads, nb),
        in_specs=[
            pl.BlockSpec((T, HD_ATTN), lambda h, i: (i, q_col + h)),
            pl.BlockSpec((L, HD_ATTN), lambda h, i: (0, h)),
            pl.BlockSpec((L, HD_ATTN), lambda h, i: (0, h)),
            pl.BlockSpec((None, nbp, HD_ATTN), lambda h, i: (h, 0, 0)),
            pl.BlockSpec((None, T, T), lambda h, i: (h, 0, 0)),
            pl.BlockSpec((None, T, T), lambda h, i: (h, 0, 0)),
            pl.BlockSpec((None, 1, T), lambda h, i: (h, 0, 0)),
        ],
        out_specs=pl.BlockSpec((T, HD_ATTN), lambda h, i: (i, h)),
        out_shape=jax.ShapeDtypeStruct((L, n_heads * HD_ATTN), F32),
        scratch_shapes=[
            pltpu.VMEM((nb, T, HD_ATTN), BF16),
            pltpu.VMEM((nb, HD_ATTN, T), BF16),
            pltpu.VMEM((T, HD_ATTN), BF16),
            pltpu.VMEM((nbp, T), F32),
            pltpu.VMEM((1, T), F32),
            pltpu.VMEM((1, T), F32),
            pltpu.VMEM((HD_ATTN, T), F32),
            pltpu.VMEM((FAR_GROUP, T, T), F32),
            pltpu.VMEM((FAR_GROUP, T, T), F32),
        ],
        compiler_params=_params(("parallel", "arbitrary")),
    )(h2, k2, v2, kmean, bias_diag, bias_prev, bias_far)


QPAD = 8
SAMPLE_BLOCKS_PER_STEP = 8


def _head_rows(ref, h, n_heads):
    return ref[pl.ds(h, ref.shape[0] // n_heads, stride=n_heads), :]


def _sample_p1_kernel(pt_ref, *refs, bps, n_heads):
    n = pl.program_id(1)
    page_refs = refs[:2 * bps]
    w_ref, lg_ref, gate_ref = refs[2 * bps:]
    w = w_ref[...].astype(BF16)
    P = page_refs[0].shape[0] // n_heads
    C = w.shape[1]
    col_head = lax.broadcasted_iota(jnp.int32, (P, C), 1) // QPAD
    for blk in range(bps):
        gsum = jnp.zeros((1, C), F32)
        for half in range(2):
            ref = page_refs[2 * blk + half]
            s = jnp.zeros((P, C), F32)
            for h in range(n_heads):
                sh = jnp.dot(_head_rows(ref, h, n_heads).astype(BF16), w, preferred_element_type=F32)
                s = jnp.where(col_head == h, sh, s)
            gsum = gsum + jnp.sum(s, axis=0, keepdims=True)
            lg_ref[blk, pl.ds(half * P, P), :] = s * (HD_ATTN ** -0.5)
        gate_ref[pl.ds(n * bps + blk, 1), :] = gsum * (1.0 / MOBA_BLOCK)


def _sample_p2_kernel(lg_ref, gate_ref, kn_ref, w_ref, bl_ref, bf_ref, bo_ref, p_ref, pn_ref, *, nbp, n_heads):
    gate = gate_ref[...]
    selm = _select_topk(gate, jnp.ones(gate.shape, jnp.bool_), 0)
    s = lg_ref[...] + selm[:, None, :]
    blk = lax.broadcasted_iota(jnp.int32, (nbp, 1, 1), 0)
    s = s + jnp.where(blk == nbp - 1, bl_ref[...][None], bf_ref[...][None])
    w = w_ref[...].astype(BF16)
    col_head = lax.broadcasted_iota(jnp.int32, (QPAD, w.shape[1]), 1) // QPAD
    sn = jnp.zeros((QPAD, w.shape[1]), F32)
    for h in range(n_heads):
        sn = jnp.where(col_head == h, jnp.dot(_head_rows(kn_ref, h, n_heads).astype(BF16), w,
                                              preferred_element_type=F32), sn)
    sn = sn * (HD_ATTN ** -0.5) + bo_ref[...]
    m = jnp.maximum(jnp.max(jnp.max(s, axis=1), axis=0, keepdims=True), jnp.max(sn, axis=0, keepdims=True))
    p = jnp.exp(s - m[None])
    pn = jnp.exp(sn - m)
    den = jnp.sum(jnp.sum(p, axis=1), axis=0, keepdims=True) + jnp.sum(pn, axis=0, keepdims=True)
    inv = 1.0 / den
    p = p * inv[None]
    for blk in range(nbp):
        p_ref[blk] = p[blk].T
    pn_ref[...] = pn * inv


def _sample_p3_kernel(pt_ref, *refs, n_heads, bps):
    n = pl.program_id(1)
    page_refs = refs[:2 * bps]
    p_ref, pn_ref, vn_ref, o_ref, acc_ref = refs[2 * bps:]
    P = page_refs[0].shape[0] // n_heads

    head_rows = lambda h: pl.ds(h * QPAD, QPAD)

    @pl.when(n == 0)
    def _():
        pn = pn_ref[...].astype(BF16)
        for h in range(n_heads):
            full = lax.dot_general(pn, _head_rows(vn_ref, h, n_heads).astype(BF16), (((0,), (0,)), ((), ())),
                                   preferred_element_type=F32)
            acc_ref[h] = full[h * QPAD:(h + 1) * QPAD]

    for h in range(n_heads):
        acc = acc_ref[h]
        for blk in range(bps):
            for half in range(2):
                p = p_ref[blk, head_rows(h), pl.ds(half * P, P)].astype(BF16)
                acc = acc + jnp.dot(p, _head_rows(page_refs[2 * blk + half], h, n_heads).astype(BF16),
                                    preferred_element_type=F32)
        acc_ref[h] = acc

    @pl.when(n == pl.num_programs(1) - 1)
    def _():
        for h in range(n_heads):
            o_ref[:, pl.ds(h * HD_ATTN, HD_ATTN)] = acc_ref[h]


def moba_sample(q, k_new, v_new, cache_k, cache_v, layer, page_table, rel_bias):
    B, L, H, HD = q.shape
    page = cache_k.shape[2]
    n_pages = page_table.shape[1]
    past = n_pages * page
    assert MOBA_BLOCK == 2 * page and past % MOBA_BLOCK == 0 and L <= QPAD
    nbp = past // MOBA_BLOCK
    C = H * QPAD
    CH = H * HD
    qp = jnp.pad(q, ((0, 0), (0, QPAD - L), (0, 0), (0, 0)))
    wq = qp.transpose(0, 3, 2, 1).reshape(B, HD, C)

    qpos = past + jnp.arange(QPAD)
    kl = (nbp - 1) * MOBA_BLOCK + jnp.arange(MOBA_BLOCK)
    bias_last = _bias_of_dist(rel_bias, qpos[None, :] - kl[:, None])
    bias_last = jnp.moveaxis(bias_last, 0, 1).reshape(MOBA_BLOCK, C)
    bias_far = jnp.repeat(rel_bias[NUM_BUCKETS - 1], QPAD).reshape(1, C)
    kown = past + jnp.arange(QPAD)
    dist_own = qpos[None, :] - kown[:, None]
    ok = (jnp.arange(QPAD)[:, None] < L) & (jnp.arange(QPAD)[None, :] < L)
    bias_own = _bias_of_dist(rel_bias, jnp.where(ok, dist_own, -1))
    bias_own = jnp.moveaxis(bias_own, 0, 1).reshape(QPAD, C)

    rows_ph = lambda t: t.reshape(t.shape[:-3] + (t.shape[-3] * H, HD))
    knp = rows_ph(jnp.pad(k_new, ((0, 0), (0, QPAD - L), (0, 0), (0, 0))))
    vnp = rows_ph(jnp.pad(v_new, ((0, 0), (0, QPAD - L), (0, 0), (0, 0))))
    cache_k, cache_v = rows_ph(cache_k), rows_ph(cache_v)

    bps = _pick(nbp, SAMPLE_BLOCKS_PER_STEP)
    page_specs = [pl.BlockSpec((None, None, page * H, HD),
                               functools.partial(lambda b, n, pt, j: (layer, pt[b, 2 * bps * n + j], 0, 0), j=j))
                  for j in range(2 * bps)]
    logits, gate = pl.pallas_call(
        functools.partial(_sample_p1_kernel, bps=bps, n_heads=H),
        grid_spec=pltpu.PrefetchScalarGridSpec(
            num_scalar_prefetch=1,
            grid=(B, nbp // bps),
            in_specs=page_specs + [pl.BlockSpec((None, HD, C), lambda b, n, pt: (b, 0, 0))],
            out_specs=[pl.BlockSpec((None, bps, MOBA_BLOCK, C), lambda b, n, pt: (b, n, 0, 0)),
                       pl.BlockSpec((None, nbp, C), lambda b, n, pt: (b, 0, 0))],
        ),
        out_shape=[jax.ShapeDtypeStruct((B, nbp, MOBA_BLOCK, C), F32),
                   jax.ShapeDtypeStruct((B, nbp, C), F32)],
        compiler_params=_params(("parallel", "arbitrary")),
    )(page_table, *([cache_k] * (2 * bps)), wq)

    p_past, p_new = pl.pallas_call(
        functools.partial(_sample_p2_kernel, nbp=nbp, n_heads=H),
        grid=(B,),
        in_specs=[
            pl.BlockSpec((None, nbp, MOBA_BLOCK, C), lambda b: (b, 0, 0, 0)),
            pl.BlockSpec((None, nbp, C), lambda b: (b, 0, 0)),
            pl.BlockSpec((None, QPAD * H, HD), lambda b: (b, 0, 0)),
            pl.BlockSpec((None, HD, C), lambda b: (b, 0, 0)),
            pl.BlockSpec((MOBA_BLOCK, C), lambda b: (0, 0)),
            pl.BlockSpec((1, C), lambda b: (0, 0)),
            pl.BlockSpec((QPAD, C), lambda b: (0, 0)),
        ],
        out_specs=[pl.BlockSpec((None, nbp, C, MOBA_BLOCK), lambda b: (b, 0, 0, 0)),
                   pl.BlockSpec((None, QPAD, C), lambda b: (b, 0, 0))],
        out_shape=[jax.ShapeDtypeStruct((B, nbp, C, MOBA_BLOCK), F32),
                   jax.ShapeDtypeStruct((B, QPAD, C), F32)],
        compiler_params=_params(("parallel",)),
    )(logits, gate, knp, wq, bias_last, bias_far, bias_own)

    o = pl.pallas_call(
        functools.partial(_sample_p3_kernel, n_heads=H, bps=bps),
        grid_spec=pltpu.PrefetchScalarGridSpec(
            num_scalar_prefetch=1,
            grid=(B, nbp // bps),
            in_specs=page_specs + [
                pl.BlockSpec((None, bps, C, MOBA_BLOCK), lambda b, n, pt: (b, n, 0, 0)),
                pl.BlockSpec((None, QPAD, C), lambda b, n, pt: (b, 0, 0)),
                pl.BlockSpec((None, QPAD * H, HD), lambda b, n, pt: (b, 0, 0))],
            out_specs=pl.BlockSpec((None, QPAD, CH), lambda b, n, pt: (b, 0, 0)),
            scratch_shapes=[pltpu.VMEM((H, QPAD, HD), F32)],
        ),
        out_shape=jax.ShapeDtypeStruct((B, QPAD, CH), F32),
        compiler_params=_params(("parallel", "arbitrary")),
    )(page_table, *([cache_v] * (2 * bps)), p_past, p_new, vnp)
    return o[:, :L]


def _head_minor(x, n_heads, axis=-1):
    axis = axis % x.ndim
    shp = x.shape
    x = x.reshape(shp[:axis] + (n_heads, shp[axis] // n_heads) + shp[axis + 1:])
    return jnp.swapaxes(x, axis, axis + 1).reshape(shp)


def _head_sum(x, n_heads):
    nt = x.shape[1] // LANES
    t = x[:, 0:LANES]
    for c in range(1, nt):
        t = t + x[:, c * LANES:(c + 1) * LANES]
    shift = n_heads
    while shift < LANES:
        t = t + pltpu.roll(t, shift, axis=1)
        shift *= 2
    return jnp.concatenate([t] * nt, axis=1)


def _rwkv_prep_kernel(k_ref, a_ref, kk_ref, ka_ref, k2_ref, na_ref, nb_ref, *, n_heads):
    k = k_ref[...]
    a = a_ref[...]
    kk = k * kk_ref[...]
    nrm = jnp.maximum(jnp.sqrt(_head_sum(kk * kk, n_heads)), 1e-12)
    kk = kk / nrm
    k2_ref[...] = k * (1.0 + (a - 1.0) * ka_ref[...])
    na_ref[...] = -kk
    nb_ref[...] = kk * a


def rwkv_prep(k, a, k_k, k_a, n_heads, *, tm=256):
    M, D = k.shape
    tm = _pick(M, tm)
    blk = pl.BlockSpec((tm, D), lambda i: (i, 0))
    row = pl.BlockSpec((1, D), lambda i: (0, 0))
    return pl.pallas_call(
        functools.partial(_rwkv_prep_kernel, n_heads=n_heads),
        grid=(M // tm,),
        in_specs=[blk, blk, row, row],
        out_specs=[blk, blk, blk],
        out_shape=[jax.ShapeDtypeStruct((M, D), F32)] * 3,
        compiler_params=_params(("parallel",)),
    )(k, a, k_k.reshape(1, D), k_a.reshape(1, D))


def _rwkv_post_kernel(y_ref, r_ref, k2_ref, v_ref, g_ref, rk_ref, lg_ref, lb_ref, o_ref, *, n_heads):
    y = y_ref[...]
    inv = 1.0 / HS_RWKV
    mu = _head_sum(y, n_heads) * inv
    yc = y - mu
    var = _head_sum(yc * yc, n_heads) * inv
    yn = yc * lax.rsqrt(var + GN_EPS) * lg_ref[...] + lb_ref[...]
    bonus = _head_sum(r_ref[...] * k2_ref[...] * rk_ref[...], n_heads) * v_ref[...]
    o_ref[...] = (yn + bonus) * g_ref[...]


def rwkv_post(y, r, k2, v, g, r_k, lnx_g, lnx_b, n_heads, *, tm=256):
    M, D = y.shape
    tm = _pick(M, tm)
    blk = pl.BlockSpec((tm, D), lambda i: (i, 0))
    row = pl.BlockSpec((1, D), lambda i: (0, 0))
    return pl.pallas_call(
        functools.partial(_rwkv_post_kernel, n_heads=n_heads),
        grid=(M // tm,),
        in_specs=[blk] * 5 + [row] * 3,
        out_specs=blk,
        out_shape=jax.ShapeDtypeStruct((M, D), F32),
        compiler_params=_params(("parallel",)),
    )(y, r, k2, v, g, r_k.reshape(1, D), lnx_g.reshape(1, D), lnx_b.reshape(1, D))


def _split3(x):
    hi = x.astype(BF16)
    r1 = x - hi.astype(F32)
    mid = r1.astype(BF16)
    lo = (r1 - mid.astype(F32)).astype(BF16)
    return hi, mid, lo


SCAN_GROUP = 16


def _scan_kernel(r_ref, d_ref, k_ref, a_ref, b_ref, v_ref, s0_ref, y_ref, so_ref, cola_ref, colb_ref, s_ref, vy_ref,
                 *, ng, nh, nvh, n_valid):
    c = pl.program_id(1)
    G = SCAN_GROUP
    klo = LANES // nh
    nt = HS_RWKV // klo

    @pl.when(c == 0)
    def _():
        s_ref[...] = s0_ref[...]

    rr = lax.broadcasted_iota(jnp.int32, (LANES, klo * LANES), 0)
    cc = lax.broadcasted_iota(jnp.int32, (LANES, klo * LANES), 1)
    spread = ((rr // nh == cc // LANES) & (rr % nh == cc % nh)).astype(BF16)
    spread3 = jnp.concatenate([spread, spread, spread], axis=0)
    ops = (r_ref, d_ref, k_ref, a_ref, b_ref)

    def build(gi, col_ref):
        rows = pl.ds(pl.multiple_of(gi * G, G), G)
        pieces = []
        for ref in ops:
            hi, mid, lo = _split3(ref[rows, :])
            for jt in range(nt):
                ls = slice(jt * LANES, (jt + 1) * LANES)
                pieces.append(jnp.concatenate([hi[:, ls], mid[:, ls], lo[:, ls]], axis=1))
        out = jnp.dot(jnp.concatenate(pieces, axis=0), spread3, preferred_element_type=F32)
        for o in range(len(ops)):
            for jt in range(nt):
                src = (o * nt + jt) * G
                for kl in range(klo):
                    dst = (o * HS_RWKV + jt * klo + kl) * G
                    col_ref[pl.ds(dst, G), :] = out[src:src + G, kl * LANES:(kl + 1) * LANES]

    def consume(gi, col_ref, ntok):
        rows = pl.ds(pl.multiple_of(gi * G, G), G)
        vy_ref[...] = v_ref[rows, :]
        for tt in range(ntok):
            tile = lambda o: col_ref[pl.ds(o * HS_RWKV * G + tt, HS_RWKV, stride=G), :]
            r_c, d_c, k_c, a_c, b_c = (tile(o) for o in range(5))
            for vh in range(nvh):
                lanes = pl.ds(vh * LANES, LANES)
                S = s_ref[vh]
                sa = jnp.sum(S * a_c, axis=0, keepdims=True)
                S = S * d_c + sa * b_c + vy_ref[pl.ds(tt, 1), lanes] * k_c
                s_ref[vh] = S
                vy_ref[pl.ds(tt, 1), lanes] = jnp.sum(S * r_c, axis=0, keepdims=True)
        y_ref[rows, :] = vy_ref[...]

    build(0, cola_ref)
    if ng % 2:
        assert ng == 1
        consume(0, cola_ref, n_valid)
    else:
        assert n_valid == G
        def pair(gp, carry):
            g0 = 2 * gp
            build(g0 + 1, colb_ref)
            consume(g0, cola_ref, G)
            build(jnp.minimum(g0 + 2, ng - 1), cola_ref)
            consume(g0 + 1, colb_ref, G)
            return carry

        lax.fori_loop(0, ng // 2, pair, 0)

    @pl.when(c == pl.num_programs(1) - 1)
    def _():
        so_ref[...] = s_ref[...]


def rwkv_scan(r, d, k, a, b, v, state, n_heads, *, tc=64):
    B, L, D = r.shape
    H = n_heads
    G = SCAN_GROUP
    vlo = LANES // H
    nvh = HS_RWKV // vlo
    assert H * vlo == LANES and nvh * vlo == HS_RWKV
    Lp = -(-L // G) * G
    n_valid = G
    if Lp != L:
        assert Lp == G
        n_valid = L
        r, d, k, a, b, v = (jnp.pad(t, ((0, 0), (0, Lp - L), (0, 0))) for t in (r, d, k, a, b, v))
    tc = _pick(Lp, tc)
    ng = tc // G
    s0 = state.reshape(B, H, nvh, vlo, HS_RWKV).transpose(0, 2, 4, 3, 1).reshape(B, nvh, HS_RWKV, LANES)
    xspec = pl.BlockSpec((None, tc, D), lambda bb, c: (bb, c, 0))
    sspec = pl.BlockSpec((None, nvh, HS_RWKV, LANES), lambda bb, c: (bb, 0, 0, 0))
    y, s1 = pl.pallas_call(
        functools.partial(_scan_kernel, ng=ng, nh=H, nvh=nvh, n_valid=n_valid),
        grid=(B, Lp // tc),
        in_specs=[xspec] * 6 + [sspec],
        out_specs=[xspec, sspec],
        out_shape=[jax.ShapeDtypeStruct((B, Lp, D), F32),
                   jax.ShapeDtypeStruct((B, nvh, HS_RWKV, LANES), F32)],
        scratch_shapes=[pltpu.VMEM((5 * HS_RWKV * G, LANES), F32),
                        pltpu.VMEM((5 * HS_RWKV * G, LANES), F32),
                        pltpu.VMEM((nvh, HS_RWKV, LANES), F32),
                        pltpu.VMEM((G, D), F32)],
        compiler_params=_params(("parallel", "arbitrary")),
    )(r, d, k, a, b, v, s0)
    s1 = s1.reshape(B, nvh, HS_RWKV, vlo, H).transpose(0, 4, 1, 3, 2).reshape(B, H, HS_RWKV, HS_RWKV)
    return y[:, :L], s1


def _conv_attn_layer(x, conv_prev, attn_fn, w_in, conv_w, conv_b, cln_g, cln_b, w_out, ln_g, ln_b, alpha,
                     c_conv, c_attn):
    B, L, D = x.shape
    x2 = x.reshape(B * L, D)
    n_hq = 2 * c_conv + c_attn
    h = matmul(x2, w_in[:, :n_hq], tm=1024).reshape(B, L, -1)
    k = matmul(x2, w_in[:, n_hq:n_hq + c_attn], tm=1024, tn=c_attn).reshape(B, L, c_attn)
    v = matmul(x2, w_in[:, n_hq + c_attn:], tm=1024, tn=c_attn).reshape(B, L, c_attn)
    c, cstate = conv_module(h, conv_prev, conv_w, conv_b, cln_g, cln_b, c_conv)
    o = attn_fn(h, k, v)
    co = jnp.concatenate([c, o], axis=-1).reshape(B * L, c_conv + c_attn)
    xn = rowmm_ln(co, w_out, x2, ln_g, ln_b, alpha).reshape(B, L, D)
    return xn, k, v, cstate


def _rwkv_layer(x, shift_prev, wkv_prev, p, ln_g, ln_b, alpha, n_heads):
    B, L, D = x.shape
    x_prev = jnp.concatenate([shift_prev[:, None], x[:, :-1]], axis=1).reshape(B * L, D)
    x2 = x.reshape(B * L, D)
    mixrow = lambda i: p["mix"][i].reshape(1, D)
    proj = lambda i, w, **kw: matmul(x2, w, xprev=x_prev, mixrow=mixrow(i), tn=2048, **kw)
    r = proj(0, p["wr"])
    lw = proj(1, p["w1"], epilogue="tanh")
    d = matmul(lw, p["w2"], bias=p["w0"].reshape(1, D), epilogue="decay", tn=2048)
    k = proj(2, p["wk"])
    v = proj(3, p["wv"])
    la = proj(4, p["a1"])
    a = matmul(la, p["a2"], bias=p["a0"].reshape(1, D), epilogue="sigmoid", tn=2048)
    lg = proj(5, p["g1"], epilogue="sigmoid")
    g = matmul(lg, p["g2"], tn=2048)
    k2, na, nb = rwkv_prep(k, a, p["kk"], p["ka"], n_heads)
    sh = lambda t: t.reshape(B, L, D)
    y, s1 = rwkv_scan(sh(r), sh(d), sh(k2), sh(na), sh(nb), sh(v), wkv_prev, n_heads)
    yg = rwkv_post(y.reshape(B * L, D), r, k2, v, g, p["rk"], p["lnx_g"], p["lnx_b"], n_heads)
    xn = rowmm_ln(yg, p["wo"], x2, ln_g, ln_b, alpha).reshape(B, L, D)
    return xn, s1, x[:, -1]


def kernel(x_prompt, x_sample, cache_k, cache_v, page_table, state_conv, state_wkv, state_shift, rel_bias, ab_w_in, ab_conv_w, ab_conv_b, ab_ln_g, ab_ln_b, ab_w_out, rw_mix, rw_wr, rw_wk, rw_wv, rw_wo, rw_w0, rw_w1, rw_w2, rw_a0, rw_a1, rw_a2, rw_g1, rw_g2, rw_kk, rw_ka, rw_rk, rw_lnx_g, rw_lnx_b, ln_g, ln_b, mlp_w1, mlp_w2):
    B, L, D = x_prompt.shape
    DB, DL, _ = x_sample.shape
    depth = ln_g.shape[0]
    alpha = (2 * depth) ** 0.25
    c_conv = ab_conv_w.shape[2]
    n_heads_attn = cache_k.shape[3]
    c_attn = n_heads_attn * HD_ATTN
    n_heads_rwkv = D // HS_RWKV
    assert B == 1
    bf = lambda w: w.astype(BF16)
    row = lambda a: a.reshape(1, -1)
    q_col = 2 * c_conv // LANES
    k_col = q_col + c_attn // LANES
    v_col = k_col + c_attn // LANES

    xp, xs = x_prompt, x_sample
    outs = {n: [] for n in ("kp", "vp", "ks", "vs", "cp", "cs", "wp", "ws", "hp", "hs")}
    for layer in range(depth):
        i = layer // 2
        g0, b0, g1, b1 = row(ln_g[layer, 0]), row(ln_b[layer, 0]), row(ln_g[layer, 1]), row(ln_b[layer, 1])
        if layer % 2 == 0:
            ab = (bf(ab_w_in[i]), ab_conv_w[i], ab_conv_b[i], ab_ln_g[i], ab_ln_b[i], bf(ab_w_out[i]), g0, b0, alpha,
                  c_conv, c_attn)
            flat = lambda t: t.reshape(L, -1)
            attn_p = lambda h, k, v: moba_prompt(flat(h), flat(k), flat(v), rel_bias, n_heads_attn, q_col)[None]
            xp, kp, vp, cp = _conv_attn_layer(xp, jnp.zeros((B, CONV_W - 1, c_conv), F32), attn_p, *ab)

            def attn_s(h, k, v):
                hd = lambda t: t.reshape(DB, DL, n_heads_attn, HD_ATTN)
                return moba_sample(hd(h[..., 2 * c_conv:]), hd(k), hd(v), cache_k, cache_v, i, page_table, rel_bias)

            xs, ks, vs, cs = _conv_attn_layer(xs, state_conv[i], attn_s, *ab)
            hd4 = lambda t: t.reshape(t.shape[0], t.shape[1], n_heads_attn, HD_ATTN)
            outs["kp"].append(hd4(kp)); outs["vp"].append(hd4(vp))
            outs["ks"].append(hd4(ks)); outs["vs"].append(hd4(vs))
            outs["cp"].append(cp); outs["cs"].append(cs)
        else:
            hm = functools.partial(_head_minor, n_heads=n_heads_rwkv)
            p = dict(mix=rw_mix[i], wr=bf(hm(rw_wr[i])), wk=bf(hm(rw_wk[i])), wv=bf(hm(rw_wv[i])),
                     wo=bf(hm(rw_wo[i], axis=0)), w0=hm(rw_w0[i]), w1=bf(rw_w1[i]), w2=bf(hm(rw_w2[i])),
                     a0=hm(rw_a0[i]), a1=bf(rw_a1[i]), a2=bf(hm(rw_a2[i])), g1=bf(rw_g1[i]), g2=bf(hm(rw_g2[i])),
                     kk=hm(rw_kk[i]), ka=hm(rw_ka[i]), rk=hm(rw_rk[i].reshape(-1)),
                     lnx_g=hm(rw_lnx_g[i]), lnx_b=hm(rw_lnx_b[i]))
            xp, wp, hp = _rwkv_layer(xp, jnp.zeros((B, D), F32),
                                     jnp.zeros((B, n_heads_rwkv, HS_RWKV, HS_RWKV), F32), p, g0, b0, alpha,
                                     n_heads_rwkv)
            xs, ws, hs = _rwkv_layer(xs, state_shift[i], state_wkv[i], p, g0, b0, alpha, n_heads_rwkv)
            outs["wp"].append(wp); outs["ws"].append(ws); outs["hp"].append(hp); outs["hs"].append(hs)
        w1b, w2b = bf(mlp_w1[layer]), bf(mlp_w2[layer])
        xp = mlp_ln(xp.reshape(B * L, D), w1b, w2b, g1, b1, alpha).reshape(B, L, D)
        xs = mlp_ln(xs.reshape(DB * DL, D), w1b, w2b, g1, b1, alpha).reshape(DB, DL, D)
    st = lambda n: jnp.stack(outs[n])
    return (xp, xs, st("kp"), st("vp"), st("ks"), st("vs"), st("cp"), st("cs"),
            st("wp"), st("ws"), st("hp"), st("hs"))
```

```python
import functools
import math

import jax
import jax.numpy as jnp
import numpy as np
from jax import lax
from jax.experimental import pallas as pl
from jax.experimental.pallas import tpu as pltpu

F32 = jnp.float32
BF16 = jnp.bfloat16
HIGHEST = lax.Precision.HIGHEST

LANES = 128
SUBLANES = 8
VMEM_LIMIT = 56 * 1024 * 1024

CONV_W = 31
HD_ATTN = 128
MOBA_BLOCK = 256
MOBA_TOPK = 3
NUM_BUCKETS = 32
MAX_DISTANCE = 128
HS_RWKV = 64
GN_EPS = 64e-5
LN_EPS = 1e-5
NEG = -1e30


def _params(sem):
    return pltpu.CompilerParams(dimension_semantics=sem, vmem_limit_bytes=VMEM_LIMIT)


def _pick(n, pref):
    if n <= pref:
        return n
    t = pref
    while n % t:
        t //= 2
    return t


def _softplus(y):
    return jnp.maximum(y, 0.0) + jnp.log(1.0 + jnp.exp(-jnp.abs(y)))


def _sigmoid(y):
    return 1.0 / (1.0 + jnp.exp(-y))


def _epilogue(acc, kind):
    if kind == "none":
        return acc
    if kind == "tanh":
        return jnp.tanh(acc)
    if kind == "sigmoid":
        return _sigmoid(acc)
    if kind == "decay":
        return jnp.exp(-jnp.exp(-_softplus(-acc) - 0.5))
    raise ValueError(kind)


def _mm_kernel(*refs, mix, shift, bias, epilogue):
    it = iter(refs)
    x_ref = next(it)
    xp_ref = next(it) if mix else None
    m_ref = next(it) if mix else None
    w_ref = next(it)
    b_ref = next(it) if bias else None
    o_ref = next(it)
    xs_ref = next(it)
    carry_ref = next(it) if shift else None

    @pl.when(pl.program_id(1) == 0)
    def _():
        x = x_ref[...]
        if shift:
            @pl.when(pl.program_id(0) == 0)
            def _():
                carry_ref[...] = jnp.broadcast_to(xp_ref[...], carry_ref.shape)

            row = lax.broadcasted_iota(jnp.int32, x.shape, 0)
            xp = jnp.where(row == 0, carry_ref[0:1, :], pltpu.roll(x, 1, axis=0))
            carry_ref[...] = jnp.broadcast_to(x[x.shape[0] - 1:, :], carry_ref.shape)
            x = x + (xp - x) * m_ref[...]
        elif mix:
            x = x + (xp_ref[...] - x) * m_ref[...]
        xs_ref[...] = x.astype(BF16)

    acc = jnp.dot(xs_ref[...], w_ref[...], preferred_element_type=F32)
    if bias:
        acc = acc + b_ref[...]
    o_ref[...] = _epilogue(acc, epilogue)


def matmul(x, w, *, xprev=None, mixrow=None, bias=None, epilogue="none", tm=512, tn=512):
    M, K = x.shape
    N = w.shape[1]
    tm = _pick(M, tm)
    tn = _pick(N, tn)
    mix = xprev is not None
    shift = mix and xprev.shape[0] == 1 and M > 1
    ins = [x]
    specs = [pl.BlockSpec((tm, K), lambda i, j: (i, 0))]
    if mix:
        ins += [xprev, mixrow]
        prev_spec = pl.BlockSpec((1, K), lambda i, j: (0, 0)) if shift else pl.BlockSpec((tm, K), lambda i, j: (i, 0))
        specs += [prev_spec, pl.BlockSpec((1, K), lambda i, j: (0, 0))]
    ins.append(w)
    specs.append(pl.BlockSpec((K, tn), lambda i, j: (0, j)))
    if bias is not None:
        ins.append(bias)
        specs.append(pl.BlockSpec((1, tn), lambda i, j: (0, j)))
    return pl.pallas_call(
        functools.partial(_mm_kernel, mix=mix, shift=shift, bias=bias is not None, epilogue=epilogue),
        grid=(M // tm, N // tn),
        in_specs=specs,
        out_specs=pl.BlockSpec((tm, tn), lambda i, j: (i, j)),
        out_shape=jax.ShapeDtypeStruct((M, N), F32),
        scratch_shapes=[pltpu.VMEM((tm, K), BF16)] + ([pltpu.VMEM((SUBLANES, K), F32)] if shift else []),
        compiler_params=_params(("arbitrary" if shift else "parallel", "arbitrary")),
    )(*ins)


def _layer_norm(z, g, b):
    mu = jnp.mean(z, axis=-1, keepdims=True)
    zc = z - mu
    var = jnp.mean(zc * zc, axis=-1, keepdims=True)
    return zc * lax.rsqrt(var + LN_EPS) * g + b


def _rowmm_ln_kernel(*refs, alpha, n_parts):
    x_refs = refs[:n_parts]
    w_ref, r_ref, g_ref, b_ref, o_ref = refs[n_parts:]
    x = jnp.concatenate([r[...].astype(BF16) for r in x_refs], axis=1)
    y = jnp.dot(x, w_ref[...], preferred_element_type=F32)
    o_ref[...] = _layer_norm(alpha * r_ref[...] + y, g_ref[...], b_ref[...])


def rowmm_ln(xs, w, resid, g, b, alpha, *, tm=512):
    M = xs[0].shape[0]
    N = w.shape[1]
    tm = _pick(M, tm)
    return pl.pallas_call(
        functools.partial(_rowmm_ln_kernel, alpha=alpha, n_parts=len(xs)),
        grid=(M // tm,),
        in_specs=[pl.BlockSpec((tm, x.shape[1]), lambda i: (i, 0)) for x in xs] + [
            pl.BlockSpec(w.shape, lambda i: (0, 0)),
            pl.BlockSpec((tm, N), lambda i: (i, 0)),
            pl.BlockSpec((1, N), lambda i: (0, 0)),
            pl.BlockSpec((1, N), lambda i: (0, 0)),
        ],
        out_specs=pl.BlockSpec((tm, N), lambda i: (i, 0)),
        out_shape=jax.ShapeDtypeStruct((M, N), F32),
        compiler_params=_params(("parallel",)),
    )(*xs, w, resid, g, b)


def _mlp_ln_kernel(x_ref, w1_ref, w2_ref, g_ref, b_ref, o_ref, xs_ref, acc_ref, *, alpha):
    f = pl.program_id(1)

    @pl.when(f == 0)
    def _():
        xs_ref[...] = x_ref[...].astype(BF16)
        acc_ref[...] = jnp.zeros_like(acc_ref)

    h = jnp.dot(xs_ref[...], w1_ref[...], preferred_element_type=F32)
    h = jnp.maximum(h, 0.0)
    h = (h * h).astype(BF16)
    acc_ref[...] += jnp.dot(h, w2_ref[...], preferred_element_type=F32)

    @pl.when(f == pl.num_programs(1) - 1)
    def _():
        o_ref[...] = _layer_norm(alpha * x_ref[...] + acc_ref[...], g_ref[...], b_ref[...])


def mlp_ln(x, w1, w2, g, b, alpha, *, tm=1024, tf=512):
    M, D = x.shape
    FF = w1.shape[1]
    tm = _pick(M, tm)
    tf = _pick(FF, tf)
    once = pl.Buffered(1)
    return pl.pallas_call(
        functools.partial(_mlp_ln_kernel, alpha=alpha),
        grid=(M // tm, FF // tf),
        in_specs=[
            pl.BlockSpec((tm, D), lambda i, f: (i, 0), pipeline_mode=once),
            pl.BlockSpec((D, tf), lambda i, f: (0, f)),
            pl.BlockSpec((tf, D), lambda i, f: (f, 0)),
            pl.BlockSpec((1, D), lambda i, f: (0, 0)),
            pl.BlockSpec((1, D), lambda i, f: (0, 0)),
        ],
        out_specs=pl.BlockSpec((tm, D), lambda i, f: (i, 0)),
        out_shape=jax.ShapeDtypeStruct((M, D), F32),
        scratch_shapes=[pltpu.VMEM((tm, D), BF16), pltpu.VMEM((tm, D), F32)],
        compiler_params=_params(("parallel", "arbitrary")),
    )(x, w1, w2, g, b)


HALO = 32


def _conv_kernel(hv_ref, hg_ref, prev_ref, w_ref, cb_ref, g_ref, b_ref, c_ref, st_ref, ue_ref, win_ref, *, tl):
    l = pl.program_id(1)
    npv = CONV_W - 1
    off = HALO - npv

    @pl.when(l == 0)
    def _():
        ue_ref[pl.ds(off, npv), :] = prev_ref[...]

    u = hv_ref[...] * _sigmoid(hg_ref[...])
    ue_ref[pl.ds(HALO, tl), :] = u
    acc = jnp.zeros_like(u) + cb_ref[...]
    if tl % SUBLANES == 0:
        for r in range(min(SUBLANES, CONV_W)):
            span = (CONV_W - 1 - r) // SUBLANES * SUBLANES
            win_ref[pl.ds(0, tl + span), :] = ue_ref[pl.ds(off + r, tl + span), :]
            for j in range(r, CONV_W, SUBLANES):
                acc = acc + win_ref[pl.ds(j - r, tl), :] * w_ref[pl.ds(j, 1), :]
    else:
        for j in range(CONV_W):
            acc = acc + ue_ref[pl.ds(off + j, tl), :] * w_ref[pl.ds(j, 1), :]
    y = _layer_norm(acc, g_ref[...], b_ref[...])
    c_ref[...] = y * _sigmoid(y)
    tail = ue_ref[pl.ds(off + tl, npv), :]
    ue_ref[pl.ds(off, npv), :] = tail

    @pl.when(l == pl.num_programs(1) - 1)
    def _():
        st_ref[...] = tail


def conv_module(h3, conv_prev, conv_w, conv_b, ln_g, ln_b, c_conv, *, tl=256):
    B, L, _ = h3.shape
    tl = _pick(L, tl)
    npv = CONV_W - 1
    row = lambda a: a.reshape(1, c_conv)
    return pl.pallas_call(
        functools.partial(_conv_kernel, tl=tl),
        grid=(B, L // tl),
        in_specs=[
            pl.BlockSpec((None, tl, c_conv), lambda b, l: (b, l, 0)),
            pl.BlockSpec((None, tl, c_conv), lambda b, l: (b, l, 1)),
            pl.BlockSpec((None, npv, c_conv), lambda b, l: (b, 0, 0)),
            pl.BlockSpec((CONV_W, c_conv), lambda b, l: (0, 0)),
            pl.BlockSpec((1, c_conv), lambda b, l: (0, 0)),
            pl.BlockSpec((1, c_conv), lambda b, l: (0, 0)),
            pl.BlockSpec((1, c_conv), lambda b, l: (0, 0)),
        ],
        out_specs=[
            pl.BlockSpec((None, tl, c_conv), lambda b, l: (b, l, 0)),
            pl.BlockSpec((None, npv, c_conv), lambda b, l: (b, 0, 0)),
        ],
        out_shape=[
            jax.ShapeDtypeStruct((B, L, c_conv), F32),
            jax.ShapeDtypeStruct((B, npv, c_conv), F32),
        ],
        scratch_shapes=[pltpu.VMEM((HALO + tl, c_conv), F32), pltpu.VMEM((HALO + tl, c_conv), F32)],
        compiler_params=_params(("arbitrary", "arbitrary")),
    )(h3, h3, conv_prev, conv_w, row(conv_b), row(ln_g), row(ln_b))


def _t5_bucket(dist):
    n = jnp.maximum(dist, 0)
    max_exact = NUM_BUCKETS // 2
    nf = jnp.maximum(n, 1).astype(F32)
    large = max_exact + (jnp.log(nf / max_exact) / math.log(MAX_DISTANCE / max_exact)
                         * (NUM_BUCKETS - max_exact)).astype(jnp.int32)
    large = jnp.minimum(large, NUM_BUCKETS - 1)
    return jnp.where(n < max_exact, n, large)


def _bias_of_dist(rel_bias, dist):
    onehot = (_t5_bucket(dist)[..., None] == jnp.arange(NUM_BUCKETS)).astype(F32)
    b = jnp.moveaxis(jnp.dot(onehot, rel_bias, precision=HIGHEST), -1, 0)
    return jnp.where(dist[None] >= 0, b, NEG)


def _kmean_kernel(k_ref, o_ref, *, nb):
    k = k_ref[...]
    km = jnp.mean(k.reshape(nb, MOBA_BLOCK, HD_ATTN), axis=1)
    o_ref[...] = jnp.zeros_like(o_ref)
    o_ref[pl.ds(0, nb), :] = km


def _select_topk(gate, valid, axis):
    idx = lax.broadcasted_iota(jnp.int32, gate.shape, axis).astype(F32)
    big = 1e9
    g = jnp.where(valid, gate, -jnp.inf)
    sel = jnp.zeros(gate.shape, jnp.bool_)
    for _ in range(MOBA_TOPK):
        mx = jnp.max(g, axis=axis, keepdims=True)
        is_max = (g == mx) & (mx > -jnp.inf)
        first = jnp.min(jnp.where(is_max, idx, big), axis=axis, keepdims=True)
        pick = idx == first
        sel = sel | pick
        g = jnp.where(pick, -jnp.inf, g)
    return jnp.where(sel, 0.0, NEG).astype(F32)


FAR_GROUP = 4


def _flash_kernel(q_ref, k_ref, v_ref, km_ref, bd_ref, bp_ref, bf_ref, o_ref,
                  kb_ref, vt_ref, qs_ref, sel_ref, m_ref, l_ref, acc_ref, sa_ref, sb_ref, *, nb):
    i = pl.program_id(1)
    T = MOBA_BLOCK

    @pl.when(i == 0)
    def _():
        def convert(n, carry):
            r0 = pl.multiple_of(n * T, T)
            kb_ref[n] = k_ref[pl.ds(r0, T), :].astype(BF16)
            vt_ref[n] = v_ref[pl.ds(r0, T), :].T.astype(BF16)
            return carry

        lax.fori_loop(0, nb, convert, 0)

    q = q_ref[...]
    qs_ref[...] = (q * (HD_ATTN ** -0.5)).astype(BF16)
    gate_t = lax.dot_general(km_ref[...], q, (((1,), (1,)), ((), ())),
                             precision=HIGHEST, preferred_element_type=F32)
    blk = lax.broadcasted_iota(jnp.int32, gate_t.shape, 0)
    sel_ref[...] = _select_topk(gate_t, blk < i, 0)
    m_ref[...] = jnp.full_like(m_ref, NEG)
    l_ref[...] = jnp.zeros_like(l_ref)
    acc_ref[...] = jnp.zeros_like(acc_ref)

    def raw_scores(j):
        return lax.dot_general(kb_ref[j], qs_ref[...], (((1,), (1,)), ((), ())), preferred_element_type=F32)

    def absorb(js, ss):
        m_old = m_ref[...]
        m_new = m_old
        for s in ss:
            m_new = jnp.maximum(m_new, jnp.max(s, axis=0, keepdims=True))
        alpha = jnp.exp(m_old - m_new)
        l_new = alpha * l_ref[...]
        acc = alpha * acc_ref[...]
        for j, s in zip(js, ss):
            p = jnp.exp(s - m_new)
            l_new = l_new + jnp.sum(p, axis=0, keepdims=True)
            acc = acc + jnp.dot(vt_ref[j], p.astype(BF16), preferred_element_type=F32)
        l_ref[...] = l_new
        acc_ref[...] = acc
        m_ref[...] = m_new

    def update(blocks):
        absorb([j for j, _ in blocks], [raw_scores(j) + extra for j, extra in blocks])

    def far_term(j):
        return bf_ref[...] + sel_ref[pl.ds(j, 1), :]

    n_far = jnp.maximum(i - 1, 0)
    n_groups = n_far // FAR_GROUP

    def fill(g, s_ref):
        for b in range(FAR_GROUP):
            s_ref[b] = raw_scores(g * FAR_GROUP + b)

    def drain(g, s_ref):
        js = [g * FAR_GROUP + b for b in range(FAR_GROUP)]
        absorb(js, [s_ref[b] + far_term(j) for b, j in enumerate(js)])

    @pl.when(n_groups > 0)
    def _():
        fill(0, sa_ref)

    def far_pair(gp, carry):
        g0 = 2 * gp
        fill(jnp.minimum(g0 + 1, n_groups - 1), sb_ref)
        drain(g0, sa_ref)
        fill(jnp.minimum(g0 + 2, n_groups - 1), sa_ref)
        drain(g0 + 1, sb_ref)
        return carry

    lax.fori_loop(0, n_groups // 2, far_pair, 0)

    @pl.when(n_groups % 2 == 1)
    def _():
        drain(n_groups - 1, sa_ref)

    n_left = n_far - n_groups * FAR_GROUP
    tail = []
    for b in range(FAR_GROUP - 1):
        j = jnp.minimum(n_groups * FAR_GROUP + b, i)
        tail.append((j, far_term(j) + jnp.where(b < n_left, 0.0, NEG)))
    j_prev = jnp.maximum(i - 1, 0)
    tail.append((j_prev, bp_ref[...] + (sel_ref[pl.ds(j_prev, 1), :] + jnp.where(i >= 1, 0.0, NEG))))
    tail.append((i, bd_ref[...]))
    update(tail)

    o_ref[...] = (acc_ref[...] / l_ref[...]).T


def moba_prompt(h2, k2, v2, rel_bias, n_heads, q_col):
    L = h2.shape[0]
    nb = L // MOBA_BLOCK
    assert L % MOBA_BLOCK == 0
    nbp = -(-nb // SUBLANES) * SUBLANES
    T = MOBA_BLOCK
    kmean = pl.pallas_call(
        functools.partial(_kmean_kernel, nb=nb),
        grid=(n_heads,),
        in_specs=[pl.BlockSpec((L, HD_ATTN), lambda h: (0, h))],
        out_specs=pl.BlockSpec((None, nbp, HD_ATTN), lambda h: (h, 0, 0)),
        out_shape=jax.ShapeDtypeStruct((n_heads, nbp, HD_ATTN), F32),
        compiler_params=_params(("parallel",)),
    )(k2)

    ki = jnp.arange(T)[:, None]
    qi = jnp.arange(T)[None, :]
    bias_diag = _bias_of_dist(rel_bias, qi - ki)
    bias_prev = _bias_of_dist(rel_bias, qi - ki + T)
    bias_far = jnp.broadcast_to(rel_bias[NUM_BUCKETS - 1][:, None, None], (n_heads, 1, T))

    return pl.pallas_call(
        functools.partial(_flash_kernel, nb=nb),
        grid=(n_heads, nb),
        in_specs=[
            pl.BlockSpec((T, HD_ATTN), lambda h, i: (i, q_col + h)),
            pl.BlockSpec((L, HD_ATTN), lambda h, i: (0, h)),
            pl.BlockSpec((L, HD_ATTN), lambda h, i: (0, h)),
            pl.BlockSpec((None, nbp, HD_ATTN), lambda h, i: (h, 0, 0)),
            pl.BlockSpec((None, T, T), lambda h, i: (h, 0, 0)),
            pl.BlockSpec((None, T, T), lambda h, i: (h, 0, 0)),
            pl.BlockSpec((None, 1, T), lambda h, i: (h, 0, 0)),
        ],
        out_specs=pl.BlockSpec((T, HD_ATTN), lambda h, i: (i, h)),
        out_shape=jax.ShapeDtypeStruct((L, n_heads * HD_ATTN), F32),
        scratch_shapes=[
            pltpu.VMEM((nb, T, HD_ATTN), BF16),
            pltpu.VMEM((nb, HD_ATTN, T), BF16),
            pltpu.VMEM((T, HD_ATTN), BF16),
            pltpu.VMEM((nbp, T), F32),
            pltpu.VMEM((1, T), F32),
            pltpu.VMEM((1, T), F32),
            pltpu.VMEM((HD_ATTN, T), F32),
            pltpu.VMEM((FAR_GROUP, T, T), F32),
            pltpu.VMEM((FAR_GROUP, T, T), F32),
        ],
        compiler_params=_params(("parallel", "arbitrary")),
    )(h2, k2, v2, kmean, bias_diag, bias_prev, bias_far)


QPAD = 8
SAMPLE_BLOCKS_PER_STEP = 8


def _head_rows(ref, h, n_heads):
    return ref[pl.ds(h, ref.shape[0] // n_heads, stride=n_heads), :]


def _sample_p1_kernel(pt_ref, *refs, bps, n_heads):
    n = pl.program_id(1)
    page_refs = refs[:2 * bps]
    w_ref, lg_ref, gate_ref = refs[2 * bps:]
    w = w_ref[...].astype(BF16)
    P = page_refs[0].shape[0] // n_heads
    C = w.shape[1]
    col_head = lax.broadcasted_iota(jnp.int32, (P, C), 1) // QPAD
    for blk in range(bps):
        gsum = jnp.zeros((1, C), F32)
        for half in range(2):
            ref = page_refs[2 * blk + half]
            s = jnp.zeros((P, C), F32)
            for h in range(n_heads):
                sh = jnp.dot(_head_rows(ref, h, n_heads).astype(BF16), w, preferred_element_type=F32)
                s = jnp.where(col_head == h, sh, s)
            gsum = gsum + jnp.sum(s, axis=0, keepdims=True)
            lg_ref[blk, pl.ds(half * P, P), :] = s * (HD_ATTN ** -0.5)
        gate_ref[pl.ds(n * bps + blk, 1), :] = gsum * (1.0 / MOBA_BLOCK)


def _sample_p2_kernel(lg_ref, gate_ref, kn_ref, w_ref, bl_ref, bf_ref, bo_ref, p_ref, pn_ref, *, nbp, n_heads):
    gate = gate_ref[...]
    selm = _select_topk(gate, jnp.ones(gate.shape, jnp.bool_), 0)
    s = lg_ref[...] + selm[:, None, :]
    blk = lax.broadcasted_iota(jnp.int32, (nbp, 1, 1), 0)
    s = s + jnp.where(blk == nbp - 1, bl_ref[...][None], bf_ref[...][None])
    w = w_ref[...].astype(BF16)
    col_head = lax.broadcasted_iota(jnp.int32, (QPAD, w.shape[1]), 1) // QPAD
    sn = jnp.zeros((QPAD, w.shape[1]), F32)
    for h in range(n_heads):
        sn = jnp.where(col_head == h, jnp.dot(_head_rows(kn_ref, h, n_heads).astype(BF16), w,
                                              preferred_element_type=F32), sn)
    sn = sn * (HD_ATTN ** -0.5) + bo_ref[...]
    m = jnp.maximum(jnp.max(jnp.max(s, axis=1), axis=0, keepdims=True), jnp.max(sn, axis=0, keepdims=True))
    p = jnp.exp(s - m[None])
    pn = jnp.exp(sn - m)
    den = jnp.sum(jnp.sum(p, axis=1), axis=0, keepdims=True) + jnp.sum(pn, axis=0, keepdims=True)
    inv = 1.0 / den
    p = p * inv[None]
    for blk in range(nbp):
        p_ref[blk] = p[blk].T
    pn_ref[...] = pn * inv


def _sample_p3_kernel(pt_ref, *refs, n_heads, bps):
    n = pl.program_id(1)
    page_refs = refs[:2 * bps]
    p_ref, pn_ref, vn_ref, o_ref, acc_ref = refs[2 * bps:]
    P = page_refs[0].shape[0] // n_heads

    head_rows = lambda h: pl.ds(h * QPAD, QPAD)

    @pl.when(n == 0)
    def _():
        pn = pn_ref[...].astype(BF16)
        for h in range(n_heads):
            full = lax.dot_general(pn, _head_rows(vn_ref, h, n_heads).astype(BF16), (((0,), (0,)), ((), ())),
                                   preferred_element_type=F32)
            acc_ref[h] = full[h * QPAD:(h + 1) * QPAD]

    for h in range(n_heads):
        acc = acc_ref[h]
        for blk in range(bps):
            for half in range(2):
                p = p_ref[blk, head_rows(h), pl.ds(half * P, P)].astype(BF16)
                acc = acc + jnp.dot(p, _head_rows(page_refs[2 * blk + half], h, n_heads).astype(BF16),
                                    preferred_element_type=F32)
        acc_ref[h] = acc

    @pl.when(n == pl.num_programs(1) - 1)
    def _():
        for h in range(n_heads):
            o_ref[:, pl.ds(h * HD_ATTN, HD_ATTN)] = acc_ref[h]


def moba_sample(q, k_new, v_new, cache_k, cache_v, layer, page_table, rel_bias):
    B, L, H, HD = q.shape
    page = cache_k.shape[2]
    n_pages = page_table.shape[1]
    past = n_pages * page
    assert MOBA_BLOCK == 2 * page and past % MOBA_BLOCK == 0 and L <= QPAD
    nbp = past // MOBA_BLOCK
    C = H * QPAD
    CH = H * HD
    qp = jnp.pad(q, ((0, 0), (0, QPAD - L), (0, 0), (0, 0)))
    wq = qp.transpose(0, 3, 2, 1).reshape(B, HD, C)

    qpos = past + jnp.arange(QPAD)
    kl = (nbp - 1) * MOBA_BLOCK + jnp.arange(MOBA_BLOCK)
    bias_last = _bias_of_dist(rel_bias, qpos[None, :] - kl[:, None])
    bias_last = jnp.moveaxis(bias_last, 0, 1).reshape(MOBA_BLOCK, C)
    bias_far = jnp.repeat(rel_bias[NUM_BUCKETS - 1], QPAD).reshape(1, C)
    kown = past + jnp.arange(QPAD)
    dist_own = qpos[None, :] - kown[:, None]
    ok = (jnp.arange(QPAD)[:, None] < L) & (jnp.arange(QPAD)[None, :] < L)
    bias_own = _bias_of_dist(rel_bias, jnp.where(ok, dist_own, -1))
    bias_own = jnp.moveaxis(bias_own, 0, 1).reshape(QPAD, C)

    rows_ph = lambda t: t.reshape(t.shape[:-3] + (t.shape[-3] * H, HD))
    knp = rows_ph(jnp.pad(k_new, ((0, 0), (0, QPAD - L), (0, 0), (0, 0))))
    vnp = rows_ph(jnp.pad(v_new, ((0, 0), (0, QPAD - L), (0, 0), (0, 0))))
    cache_k, cache_v = rows_ph(cache_k), rows_ph(cache_v)

    bps = _pick(nbp, SAMPLE_BLOCKS_PER_STEP)
    page_specs = [pl.BlockSpec((None, None, page * H, HD),
                               functools.partial(lambda b, n, pt, j: (layer, pt[b, 2 * bps * n + j], 0, 0), j=j))
                  for j in range(2 * bps)]
    logits, gate = pl.pallas_call(
        functools.partial(_sample_p1_kernel, bps=bps, n_heads=H),
        grid_spec=pltpu.PrefetchScalarGridSpec(
            num_scalar_prefetch=1,
            grid=(B, nbp // bps),
            in_specs=page_specs + [pl.BlockSpec((None, HD, C), lambda b, n, pt: (b, 0, 0))],
            out_specs=[pl.BlockSpec((None, bps, MOBA_BLOCK, C), lambda b, n, pt: (b, n, 0, 0)),
                       pl.BlockSpec((None, nbp, C), lambda b, n, pt: (b, 0, 0))],
        ),
        out_shape=[jax.ShapeDtypeStruct((B, nbp, MOBA_BLOCK, C), F32),
                   jax.ShapeDtypeStruct((B, nbp, C), F32)],
        compiler_params=_params(("parallel", "arbitrary")),
    )(page_table, *([cache_k] * (2 * bps)), wq)

    p_past, p_new = pl.pallas_call(
        functools.partial(_sample_p2_kernel, nbp=nbp, n_heads=H),
        grid=(B,),
        in_specs=[
            pl.BlockSpec((None, nbp, MOBA_BLOCK, C), lambda b: (b, 0, 0, 0)),
            pl.BlockSpec((None, nbp, C), lambda b: (b, 0, 0)),
            pl.BlockSpec((None, QPAD * H, HD), lambda b: (b, 0, 0)),
            pl.BlockSpec((None, HD, C), lambda b: (b, 0, 0)),
            pl.BlockSpec((MOBA_BLOCK, C), lambda b: (0, 0)),
            pl.BlockSpec((1, C), lambda b: (0, 0)),
            pl.BlockSpec((QPAD, C), lambda b: (0, 0)),
        ],
        out_specs=[pl.BlockSpec((None, nbp, C, MOBA_BLOCK), lambda b: (b, 0, 0, 0)),
                   pl.BlockSpec((None, QPAD, C), lambda b: (b, 0, 0))],
        out_shape=[jax.ShapeDtypeStruct((B, nbp, C, MOBA_BLOCK), F32),
                   jax.ShapeDtypeStruct((B, QPAD, C), F32)],
        compiler_params=_params(("parallel",)),
    )(logits, gate, knp, wq, bias_last, bias_far, bias_own)

    o = pl.pallas_call(
        functools.partial(_sample_p3_kernel, n_heads=H, bps=bps),
        grid_spec=pltpu.PrefetchScalarGridSpec(
            num_scalar_prefetch=1,
            grid=(B, nbp // bps),
            in_specs=page_specs + [
                pl.BlockSpec((None, bps, C, MOBA_BLOCK), lambda b, n, pt: (b, n, 0, 0)),
                pl.BlockSpec((None, QPAD, C), lambda b, n, pt: (b, 0, 0)),
                pl.BlockSpec((None, QPAD * H, HD), lambda b, n, pt: (b, 0, 0))],
            out_specs=pl.BlockSpec((None, QPAD, CH), lambda b, n, pt: (b, 0, 0)),
            scratch_shapes=[pltpu.VMEM((H, QPAD, HD), F32)],
        ),
        out_shape=jax.ShapeDtypeStruct((B, QPAD, CH), F32),
        compiler_params=_params(("parallel", "arbitrary")),
    )(page_table, *([cache_v] * (2 * bps)), p_past, p_new, vnp)
    return o[:, :L]


def _head_minor(x, n_heads, axis=-1):
    axis = axis % x.ndim
    shp = x.shape
    x = x.reshape(shp[:axis] + (n_heads, shp[axis] // n_heads) + shp[axis + 1:])
    return jnp.swapaxes(x, axis, axis + 1).reshape(shp)


def _head_sum(x, n_heads):
    nt = x.shape[1] // LANES
    t = x[:, 0:LANES]
    for c in range(1, nt):
        t = t + x[:, c * LANES:(c + 1) * LANES]
    shift = n_heads
    while shift < LANES:
        t = t + pltpu.roll(t, shift, axis=1)
        shift *= 2
    return jnp.concatenate([t] * nt, axis=1)


def _rwkv_prep_kernel(k_ref, a_ref, kk_ref, ka_ref, k2_ref, na_ref, nb_ref, *, n_heads):
    k = k_ref[...]
    a = a_ref[...]
    kk = k * kk_ref[...]
    nrm = jnp.maximum(jnp.sqrt(_head_sum(kk * kk, n_heads)), 1e-12)
    kk = kk / nrm
    k2_ref[...] = k * (1.0 + (a - 1.0) * ka_ref[...])
    na_ref[...] = -kk
    nb_ref[...] = kk * a


def rwkv_prep(k, a, k_k, k_a, n_heads, *, tm=256):
    M, D = k.shape
    tm = _pick(M, tm)
    blk = pl.BlockSpec((tm, D), lambda i: (i, 0))
    row = pl.BlockSpec((1, D), lambda i: (0, 0))
    return pl.pallas_call(
        functools.partial(_rwkv_prep_kernel, n_heads=n_heads),
        grid=(M // tm,),
        in_specs=[blk, blk, row, row],
        out_specs=[blk, blk, blk],
        out_shape=[jax.ShapeDtypeStruct((M, D), F32)] * 3,
        compiler_params=_params(("parallel",)),
    )(k, a, k_k.reshape(1, D), k_a.reshape(1, D))


def _rwkv_post_kernel(y_ref, r_ref, k2_ref, v_ref, g_ref, rk_ref, lg_ref, lb_ref, o_ref, *, n_heads):
    y = y_ref[...]
    inv = 1.0 / HS_RWKV
    mu = _head_sum(y, n_heads) * inv
    yc = y - mu
    var = _head_sum(yc * yc, n_heads) * inv
    yn = yc * lax.rsqrt(var + GN_EPS) * lg_ref[...] + lb_ref[...]
    bonus = _head_sum(r_ref[...] * k2_ref[...] * rk_ref[...], n_heads) * v_ref[...]
    o_ref[...] = (yn + bonus) * g_ref[...]


def rwkv_post(y, r, k2, v, g, r_k, lnx_g, lnx_b, n_heads, *, tm=256):
    M, D = y.shape
    tm = _pick(M, tm)
    blk = pl.BlockSpec((tm, D), lambda i: (i, 0))
    row = pl.BlockSpec((1, D), lambda i: (0, 0))
    return pl.pallas_call(
        functools.partial(_rwkv_post_kernel, n_heads=n_heads),
        grid=(M // tm,),
        in_specs=[blk] * 5 + [row] * 3,
        out_specs=blk,
        out_shape=jax.ShapeDtypeStruct((M, D), F32),
        compiler_params=_params(("parallel",)),
    )(y, r, k2, v, g, r_k.reshape(1, D), lnx_g.reshape(1, D), lnx_b.reshape(1, D))


def _split3(x):
    hi = x.astype(BF16)
    r1 = x - hi.astype(F32)
    mid = r1.astype(BF16)
    lo = (r1 - mid.astype(F32)).astype(BF16)
    return hi, mid, lo


SCAN_GROUP = 16


def _scan_kernel(r_ref, d_ref, k_ref, a_ref, b_ref, v_ref, s0_ref, y_ref, so_ref, cola_ref, colb_ref, s_ref, vy_ref,
                 *, ng, nh, nvh, n_valid):
    c = pl.program_id(1)
    G = SCAN_GROUP
    klo = LANES // nh
    nt = HS_RWKV // klo

    @pl.when(c == 0)
    def _():
        s_ref[...] = s0_ref[...]

    rr = lax.broadcasted_iota(jnp.int32, (LANES, klo * LANES), 0)
    cc = lax.broadcasted_iota(jnp.int32, (LANES, klo * LANES), 1)
    spread = ((rr // nh == cc // LANES) & (rr % nh == cc % nh)).astype(BF16)
    spread3 = jnp.concatenate([spread, spread, spread], axis=0)
    ops = (r_ref, d_ref, k_ref, a_ref, b_ref)

    def build(gi, col_ref):
        rows = pl.ds(pl.multiple_of(gi * G, G), G)
        pieces = []
        for ref in ops:
            hi, mid, lo = _split3(ref[rows, :])
            for jt in range(nt):
                ls = slice(jt * LANES, (jt + 1) * LANES)
                pieces.append(jnp.concatenate([hi[:, ls], mid[:, ls], lo[:, ls]], axis=1))
        out = jnp.dot(jnp.concatenate(pieces, axis=0), spread3, preferred_element_type=F32)
        for o in range(len(ops)):
            for jt in range(nt):
                src = (o * nt + jt) * G
                for kl in range(klo):
                    dst = (o * HS_RWKV + jt * klo + kl) * G
                    col_ref[pl.ds(dst, G), :] = out[src:src + G, kl * LANES:(kl + 1) * LANES]

    def consume(gi, col_ref, ntok):
        rows = pl.ds(pl.multiple_of(gi * G, G), G)
        vy_ref[...] = v_ref[rows, :]
        for tt in range(ntok):
            tile = lambda o: col_ref[pl.ds(o * HS_RWKV * G + tt, HS_RWKV, stride=G), :]
            r_c, d_c, k_c, a_c, b_c = (tile(o) for o in range(5))
            for vh in range(nvh):
                lanes = pl.ds(vh * LANES, LANES)
                S = s_ref[vh]
                sa = jnp.sum(S * a_c, axis=0, keepdims=True)
                S = S * d_c + sa * b_c + vy_ref[pl.ds(tt, 1), lanes] * k_c
                s_ref[vh] = S
                vy_ref[pl.ds(tt, 1), lanes] = jnp.sum(S * r_c, axis=0, keepdims=True)
        y_ref[rows, :] = vy_ref[...]

    build(0, cola_ref)
    if ng % 2:
        assert ng == 1
        consume(0, cola_ref, n_valid)
    else:
        assert n_valid == G
        def pair(gp, carry):
            g0 = 2 * gp
            build(g0 + 1, colb_ref)
            consume(g0, cola_ref, G)
            build(jnp.minimum(g0 + 2, ng - 1), cola_ref)
            consume(g0 + 1, colb_ref, G)
            return carry

        lax.fori_loop(0, ng // 2, pair, 0)

    @pl.when(c == pl.num_programs(1) - 1)
    def _():
        so_ref[...] = s_ref[...]


def rwkv_scan(r, d, k, a, b, v, state, n_heads, *, tc=256):
    B, L, D = r.shape
    H = n_heads
    G = SCAN_GROUP
    vlo = LANES // H
    nvh = HS_RWKV // vlo
    assert H * vlo == LANES and nvh * vlo == HS_RWKV
    Lp = -(-L // G) * G
    n_valid = G
    if Lp != L:
        assert Lp == G
        n_valid = L
        r, d, k, a, b, v = (jnp.pad(t, ((0, 0), (0, Lp - L), (0, 0))) for t in (r, d, k, a, b, v))
    tc = _pick(Lp, tc)
    ng = tc // G
    s0 = state.reshape(B, H, nvh, vlo, HS_RWKV).transpose(0, 2, 4, 3, 1).reshape(B, nvh, HS_RWKV, LANES)
    xspec = pl.BlockSpec((None, tc, D), lambda bb, c: (bb, c, 0))
    sspec = pl.BlockSpec((None, nvh, HS_RWKV, LANES), lambda bb, c: (bb, 0, 0, 0))
    y, s1 = pl.pallas_call(
        functools.partial(_scan_kernel, ng=ng, nh=H, nvh=nvh, n_valid=n_valid),
        grid=(B, Lp // tc),
        in_specs=[xspec] * 6 + [sspec],
        out_specs=[xspec, sspec],
        out_shape=[jax.ShapeDtypeStruct((B, Lp, D), F32),
                   jax.ShapeDtypeStruct((B, nvh, HS_RWKV, LANES), F32)],
        scratch_shapes=[pltpu.VMEM((5 * HS_RWKV * G, LANES), F32),
                        pltpu.VMEM((5 * HS_RWKV * G, LANES), F32),
                        pltpu.VMEM((nvh, HS_RWKV, LANES), F32),
                        pltpu.VMEM((G, D), F32)],
        compiler_params=_params(("parallel", "arbitrary")),
    )(r, d, k, a, b, v, s0)
    s1 = s1.reshape(B, nvh, HS_RWKV, vlo, H).transpose(0, 4, 1, 3, 2).reshape(B, H, HS_RWKV, HS_RWKV)
    return y[:, :L], s1


def _conv_attn_layer(x, conv_prev, attn_fn, w_in, conv_w, conv_b, cln_g, cln_b, w_out, ln_g, ln_b, alpha,
                     c_conv, c_attn):
    B, L, D = x.shape
    x2 = x.reshape(B * L, D)
    n_hq = 2 * c_conv + c_attn
    h = matmul(x2, w_in[:, :n_hq], tm=1024).reshape(B, L, -1)
    k = matmul(x2, w_in[:, n_hq:n_hq + c_attn], tm=1024, tn=c_attn).reshape(B, L, c_attn)
    v = matmul(x2, w_in[:, n_hq + c_attn:], tm=1024, tn=c_attn).reshape(B, L, c_attn)
    c, cstate = conv_module(h, conv_prev, conv_w, conv_b, cln_g, cln_b, c_conv)
    o = attn_fn(h, k, v)
    xn = rowmm_ln([c.reshape(B * L, c_conv), o.reshape(B * L, c_attn)], w_out, x2, ln_g, ln_b, alpha).reshape(B, L, D)
    return xn, k, v, cstate


def _rwkv_layer(x, shift_prev, wkv_prev, p, ln_g, ln_b, alpha, n_heads):
    B, L, D = x.shape
    if B == 1:
        x_prev = shift_prev
    else:
        x_prev = jnp.concatenate([shift_prev[:, None], x[:, :-1]], axis=1).reshape(B * L, D)
    x2 = x.reshape(B * L, D)
    mixrow = lambda i: p["mix"][i].reshape(1, D)
    proj = lambda i, w, **kw: matmul(x2, w, xprev=x_prev, mixrow=mixrow(i), tn=2048, **kw)
    r = proj(0, p["wr"])
    lw = proj(1, p["w1"], epilogue="tanh")
    d = matmul(lw, p["w2"], bias=p["w0"].reshape(1, D), epilogue="decay", tn=2048)
    k = proj(2, p["wk"])
    v = proj(3, p["wv"])
    la = proj(4, p["a1"])
    a = matmul(la, p["a2"], bias=p["a0"].reshape(1, D), epilogue="sigmoid", tn=2048)
    lg = proj(5, p["g1"], epilogue="sigmoid")
    g = matmul(lg, p["g2"], tn=2048)
    k2, na, nb = rwkv_prep(k, a, p["kk"], p["ka"], n_heads)
    sh = lambda t: t.reshape(B, L, D)
    y, s1 = rwkv_scan(sh(r), sh(d), sh(k2), sh(na), sh(nb), sh(v), wkv_prev, n_heads)
    yg = rwkv_post(y.reshape(B * L, D), r, k2, v, g, p["rk"], p["lnx_g"], p["lnx_b"], n_heads)
    xn = rowmm_ln([yg], p["wo"], x2, ln_g, ln_b, alpha).reshape(B, L, D)
    return xn, s1, x[:, -1]


def kernel(x_prompt, x_sample, cache_k, cache_v, page_table, state_conv, state_wkv, state_shift, rel_bias, ab_w_in, ab_conv_w, ab_conv_b, ab_ln_g, ab_ln_b, ab_w_out, rw_mix, rw_wr, rw_wk, rw_wv, rw_wo, rw_w0, rw_w1, rw_w2, rw_a0, rw_a1, rw_a2, rw_g1, rw_g2, rw_kk, rw_ka, rw_rk, rw_lnx_g, rw_lnx_b, ln_g, ln_b, mlp_w1, mlp_w2):
    B, L, D = x_prompt.shape
    DB, DL, _ = x_sample.shape
    depth = ln_g.shape[0]
    alpha = (2 * depth) ** 0.25
    c_conv = ab_conv_w.shape[2]
    n_heads_attn = cache_k.shape[3]
    c_attn = n_heads_attn * HD_ATTN
    n_heads_rwkv = D // HS_RWKV
    assert B == 1
    bf = lambda w: w.astype(BF16)
    row = lambda a: a.reshape(1, -1)
    q_col = 2 * c_conv // LANES
    k_col = q_col + c_attn // LANES
    v_col = k_col + c_attn // LANES

    xp, xs = x_prompt, x_sample
    outs = {n: [] for n in ("kp", "vp", "ks", "vs", "cp", "cs", "wp", "ws", "hp", "hs")}
    for layer in range(depth):
        i = layer // 2
        g0, b0, g1, b1 = row(ln_g[layer, 0]), row(ln_b[layer, 0]), row(ln_g[layer, 1]), row(ln_b[layer, 1])
        if layer % 2 == 0:
            ab = (bf(ab_w_in[i]), ab_conv_w[i], ab_conv_b[i], ab_ln_g[i], ab_ln_b[i], bf(ab_w_out[i]), g0, b0, alpha,
                  c_conv, c_attn)
            flat = lambda t: t.reshape(L, -1)
            attn_p = lambda h, k, v: moba_prompt(flat(h), flat(k), flat(v), rel_bias, n_heads_attn, q_col)[None]
            xp, kp, vp, cp = _conv_attn_layer(xp, jnp.zeros((B, CONV_W - 1, c_conv), F32), attn_p, *ab)

            def attn_s(h, k, v):
                hd = lambda t: t.reshape(DB, DL, n_heads_attn, HD_ATTN)
                return moba_sample(hd(h[..., 2 * c_conv:]), hd(k), hd(v), cache_k, cache_v, i, page_table, rel_bias)

            xs, ks, vs, cs = _conv_attn_layer(xs, state_conv[i], attn_s, *ab)
            hd4 = lambda t: t.reshape(t.shape[0], t.shape[1], n_heads_attn, HD_ATTN)
            outs["kp"].append(hd4(kp)); outs["vp"].append(hd4(vp))
            outs["ks"].append(hd4(ks)); outs["vs"].append(hd4(vs))
            outs["cp"].append(cp); outs["cs"].append(cs)
        else:
            hm = functools.partial(_head_minor, n_heads=n_heads_rwkv)
            p = dict(mix=rw_mix[i], wr=bf(hm(rw_wr[i])), wk=bf(hm(rw_wk[i])), wv=bf(hm(rw_wv[i])),
                     wo=bf(hm(rw_wo[i], axis=0)), w0=hm(rw_w0[i]), w1=bf(rw_w1[i]), w2=bf(hm(rw_w2[i])),
                     a0=hm(rw_a0[i]), a1=bf(rw_a1[i]), a2=bf(hm(rw_a2[i])), g1=bf(rw_g1[i]), g2=bf(hm(rw_g2[i])),
                     kk=hm(rw_kk[i]), ka=hm(rw_ka[i]), rk=hm(rw_rk[i].reshape(-1)),
                     lnx_g=hm(rw_lnx_g[i]), lnx_b=hm(rw_lnx_b[i]))
            xp, wp, hp = _rwkv_layer(xp, jnp.zeros((B, D), F32),
                                     jnp.zeros((B, n_heads_rwkv, HS_RWKV, HS_RWKV), F32), p, g0, b0, alpha,
                                     n_heads_rwkv)
            xs, ws, hs = _rwkv_layer(xs, state_shift[i], state_wkv[i], p, g0, b0, alpha, n_heads_rwkv)
            outs["wp"].append(wp); outs["ws"].append(ws); outs["hp"].append(hp); outs["hs"].append(hs)
        w1b, w2b = bf(mlp_w1[layer]), bf(mlp_w2[layer])
        xp = mlp_ln(xp.reshape(B * L, D), w1b, w2b, g1, b1, alpha).reshape(B, L, D)
        xs = mlp_ln(xs.reshape(DB * DL, D), w1b, w2b, g1, b1, alpha).reshape(DB, DL, D)
    st = lambda n: jnp.stack(outs[n])
    return (xp, xs, st("kp"), st("vp"), st("ks"), st("vs"), st("cp"), st("cs"),
            st("wp"), st("ws"), st("hp"), st("hs"))
```

```python
import functools
import math

import jax
import jax.numpy as jnp
import numpy as np
from jax import lax
from jax.experimental import pallas as pl
from jax.experimental.pallas import tpu as pltpu

F32 = jnp.float32
BF16 = jnp.bfloat16
HIGHEST = lax.Precision.HIGHEST

LANES = 128
SUBLANES = 8
VMEM_LIMIT = 56 * 1024 * 1024

CONV_W = 31
HD_ATTN = 128
MOBA_BLOCK = 256
MOBA_TOPK = 3
NUM_BUCKETS = 32
MAX_DISTANCE = 128
HS_RWKV = 64
GN_EPS = 64e-5
LN_EPS = 1e-5
NEG = -1e30


def _params(sem):
    return pltpu.CompilerParams(dimension_semantics=sem, vmem_limit_bytes=VMEM_LIMIT)


def _pick(n, pref):
    if n <= pref:
        return n
    t = pref
    while n % t:
        t //= 2
    return t


def _softplus(y):
    return jnp.maximum(y, 0.0) + jnp.log(1.0 + jnp.exp(-jnp.abs(y)))


def _sigmoid(y):
    return 1.0 / (1.0 + jnp.exp(-y))


def _epilogue(acc, kind):
    if kind == "none":
        return acc
    if kind == "tanh":
        return jnp.tanh(acc)
    if kind == "sigmoid":
        return _sigmoid(acc)
    if kind == "decay":
        return jnp.exp(-jnp.exp(-_softplus(-acc) - 0.5))
    raise ValueError(kind)


def _mm_kernel(*refs, mix, shift, bias, epilogue):
    it = iter(refs)
    x_ref = next(it)
    xp_ref = next(it) if mix else None
    m_ref = next(it) if mix else None
    w_ref = next(it)
    b_ref = next(it) if bias else None
    o_ref = next(it)
    xs_ref = next(it)
    carry_ref = next(it) if shift else None

    @pl.when(pl.program_id(1) == 0)
    def _():
        x = x_ref[...]
        if shift:
            @pl.when(pl.program_id(0) == 0)
            def _():
                carry_ref[...] = jnp.broadcast_to(xp_ref[...], carry_ref.shape)

            row = lax.broadcasted_iota(jnp.int32, x.shape, 0)
            xp = jnp.where(row == 0, carry_ref[0:1, :], pltpu.roll(x, 1, axis=0))
            carry_ref[...] = jnp.broadcast_to(x[x.shape[0] - 1:, :], carry_ref.shape)
            x = x + (xp - x) * m_ref[...]
        elif mix:
            x = x + (xp_ref[...] - x) * m_ref[...]
        xs_ref[...] = x.astype(BF16)

    acc = jnp.dot(xs_ref[...], w_ref[...], preferred_element_type=F32)
    if bias:
        acc = acc + b_ref[...]
    o_ref[...] = _epilogue(acc, epilogue)


def matmul(x, w, *, xprev=None, mixrow=None, bias=None, epilogue="none", cols=None, tm=512, tn=512):
    M, K = x.shape
    col0, N = (0, w.shape[1]) if cols is None else cols
    tm = _pick(M, tm)
    tn = _pick(N, tn)
    assert col0 % tn == 0
    jb = col0 // tn
    mix = xprev is not None
    shift = mix and xprev.shape[0] == 1 and M > 1
    ins = [x]
    specs = [pl.BlockSpec((tm, K), lambda i, j: (i, 0))]
    if mix:
        ins += [xprev, mixrow]
        prev_spec = pl.BlockSpec((1, K), lambda i, j: (0, 0)) if shift else pl.BlockSpec((tm, K), lambda i, j: (i, 0))
        specs += [prev_spec, pl.BlockSpec((1, K), lambda i, j: (0, 0))]
    ins.append(w)
    specs.append(pl.BlockSpec((K, tn), lambda i, j: (0, jb + j)))
    if bias is not None:
        ins.append(bias)
        specs.append(pl.BlockSpec((1, tn), lambda i, j: (0, j)))
    return pl.pallas_call(
        functools.partial(_mm_kernel, mix=mix, shift=shift, bias=bias is not None, epilogue=epilogue),
        grid=(M // tm, N // tn),
        in_specs=specs,
        out_specs=pl.BlockSpec((tm, tn), lambda i, j: (i, j)),
        out_shape=jax.ShapeDtypeStruct((M, N), F32),
        scratch_shapes=[pltpu.VMEM((tm, K), BF16)] + ([pltpu.VMEM((SUBLANES, K), F32)] if shift else []),
        compiler_params=_params(("arbitrary" if shift else "parallel", "arbitrary")),
    )(*ins)


def _layer_norm(z, g, b):
    mu = jnp.mean(z, axis=-1, keepdims=True)
    zc = z - mu
    var = jnp.mean(zc * zc, axis=-1, keepdims=True)
    return zc * lax.rsqrt(var + LN_EPS) * g + b


def _rowmm_ln_kernel(*refs, alpha, n_parts):
    x_refs = refs[:n_parts]
    w_ref, r_ref, g_ref, b_ref, o_ref = refs[n_parts:]
    x = jnp.concatenate([r[...].astype(BF16) for r in x_refs], axis=1)
    y = jnp.dot(x, w_ref[...], preferred_element_type=F32)
    o_ref[...] = _layer_norm(alpha * r_ref[...] + y, g_ref[...], b_ref[...])


def rowmm_ln(xs, w, resid, g, b, alpha, *, tm=512):
    M = xs[0].shape[0]
    N = w.shape[1]
    tm = _pick(M, tm)
    return pl.pallas_call(
        functools.partial(_rowmm_ln_kernel, alpha=alpha, n_parts=len(xs)),
        grid=(M // tm,),
        in_specs=[pl.BlockSpec((tm, x.shape[1]), lambda i: (i, 0)) for x in xs] + [
            pl.BlockSpec(w.shape, lambda i: (0, 0)),
            pl.BlockSpec((tm, N), lambda i: (i, 0)),
            pl.BlockSpec((1, N), lambda i: (0, 0)),
            pl.BlockSpec((1, N), lambda i: (0, 0)),
        ],
        out_specs=pl.BlockSpec((tm, N), lambda i: (i, 0)),
        out_shape=jax.ShapeDtypeStruct((M, N), F32),
        compiler_params=_params(("parallel",)),
    )(*xs, w, resid, g, b)


def _mlp_ln_kernel(x_ref, w1_ref, w2_ref, g_ref, b_ref, o_ref, xs_ref, acc_ref, *, alpha):
    f = pl.program_id(1)

    @pl.when(f == 0)
    def _():
        xs_ref[...] = x_ref[...].astype(BF16)
        acc_ref[...] = jnp.zeros_like(acc_ref)

    h = jnp.dot(xs_ref[...], w1_ref[...], preferred_element_type=F32)
    h = jnp.maximum(h, 0.0)
    h = (h * h).astype(BF16)
    acc_ref[...] += jnp.dot(h, w2_ref[...], preferred_element_type=F32)

    @pl.when(f == pl.num_programs(1) - 1)
    def _():
        o_ref[...] = _layer_norm(alpha * x_ref[...] + acc_ref[...], g_ref[...], b_ref[...])


def mlp_ln(x, w1, w2, g, b, alpha, *, tm=1024, tf=512):
    M, D = x.shape
    FF = w1.shape[1]
    tm = _pick(M, tm)
    tf = _pick(FF, tf)
    once = pl.Buffered(1)
    return pl.pallas_call(
        functools.partial(_mlp_ln_kernel, alpha=alpha),
        grid=(M // tm, FF // tf),
        in_specs=[
            pl.BlockSpec((tm, D), lambda i, f: (i, 0), pipeline_mode=once),
            pl.BlockSpec((D, tf), lambda i, f: (0, f)),
            pl.BlockSpec((tf, D), lambda i, f: (f, 0)),
            pl.BlockSpec((1, D), lambda i, f: (0, 0)),
            pl.BlockSpec((1, D), lambda i, f: (0, 0)),
        ],
        out_specs=pl.BlockSpec((tm, D), lambda i, f: (i, 0)),
        out_shape=jax.ShapeDtypeStruct((M, D), F32),
        scratch_shapes=[pltpu.VMEM((tm, D), BF16), pltpu.VMEM((tm, D), F32)],
        compiler_params=_params(("parallel", "arbitrary")),
    )(x, w1, w2, g, b)


HALO = 32


def _conv_kernel(hv_ref, hg_ref, prev_ref, w_ref, cb_ref, g_ref, b_ref, c_ref, st_ref, ue_ref, win_ref, *, tl):
    l = pl.program_id(1)
    npv = CONV_W - 1
    off = HALO - npv

    @pl.when(l == 0)
    def _():
        ue_ref[pl.ds(off, npv), :] = prev_ref[...]

    u = hv_ref[...] * _sigmoid(hg_ref[...])
    ue_ref[pl.ds(HALO, tl), :] = u
    acc = jnp.zeros_like(u) + cb_ref[...]
    if tl % SUBLANES == 0:
        for r in range(min(SUBLANES, CONV_W)):
            span = (CONV_W - 1 - r) // SUBLANES * SUBLANES
            win_ref[pl.ds(0, tl + span), :] = ue_ref[pl.ds(off + r, tl + span), :]
            for j in range(r, CONV_W, SUBLANES):
                acc = acc + win_ref[pl.ds(j - r, tl), :] * w_ref[pl.ds(j, 1), :]
    else:
        for j in range(CONV_W):
            acc = acc + ue_ref[pl.ds(off + j, tl), :] * w_ref[pl.ds(j, 1), :]
    y = _layer_norm(acc, g_ref[...], b_ref[...])
    c_ref[...] = y * _sigmoid(y)
    tail = ue_ref[pl.ds(off + tl, npv), :]
    ue_ref[pl.ds(off, npv), :] = tail

    @pl.when(l == pl.num_programs(1) - 1)
    def _():
        st_ref[...] = tail


def conv_module(h3, conv_prev, conv_w, conv_b, ln_g, ln_b, c_conv, *, tl=256):
    B, L, _ = h3.shape
    tl = _pick(L, tl)
    npv = CONV_W - 1
    row = lambda a: a.reshape(1, c_conv)
    return pl.pallas_call(
        functools.partial(_conv_kernel, tl=tl),
        grid=(B, L // tl),
        in_specs=[
            pl.BlockSpec((None, tl, c_conv), lambda b, l: (b, l, 0)),
            pl.BlockSpec((None, tl, c_conv), lambda b, l: (b, l, 1)),
            pl.BlockSpec((None, npv, c_conv), lambda b, l: (b, 0, 0)),
            pl.BlockSpec((CONV_W, c_conv), lambda b, l: (0, 0)),
            pl.BlockSpec((1, c_conv), lambda b, l: (0, 0)),
            pl.BlockSpec((1, c_conv), lambda b, l: (0, 0)),
            pl.BlockSpec((1, c_conv), lambda b, l: (0, 0)),
        ],
        out_specs=[
            pl.BlockSpec((None, tl, c_conv), lambda b, l: (b, l, 0)),
            pl.BlockSpec((None, npv, c_conv), lambda b, l: (b, 0, 0)),
        ],
        out_shape=[
            jax.ShapeDtypeStruct((B, L, c_conv), F32),
            jax.ShapeDtypeStruct((B, npv, c_conv), F32),
        ],
        scratch_shapes=[pltpu.VMEM((HALO + tl, c_conv), F32), pltpu.VMEM((HALO + tl, c_conv), F32)],
        compiler_params=_params(("arbitrary", "arbitrary")),
    )(h3, h3, conv_prev, conv_w, row(conv_b), row(ln_g), row(ln_b))


def _t5_bucket(dist):
    n = jnp.maximum(dist, 0)
    max_exact = NUM_BUCKETS // 2
    nf = jnp.maximum(n, 1).astype(F32)
    large = max_exact + (jnp.log(nf / max_exact) / math.log(MAX_DISTANCE / max_exact)
                         * (NUM_BUCKETS - max_exact)).astype(jnp.int32)
    large = jnp.minimum(large, NUM_BUCKETS - 1)
    return jnp.where(n < max_exact, n, large)


def _bias_of_dist(rel_bias, dist):
    onehot = (_t5_bucket(dist)[..., None] == jnp.arange(NUM_BUCKETS)).astype(F32)
    b = jnp.moveaxis(jnp.dot(onehot, rel_bias, precision=HIGHEST), -1, 0)
    return jnp.where(dist[None] >= 0, b, NEG)


def _kmean_kernel(k_ref, o_ref, *, nb):
    k = k_ref[...]
    km = jnp.mean(k.reshape(nb, MOBA_BLOCK, HD_ATTN), axis=1)
    o_ref[...] = jnp.zeros_like(o_ref)
    o_ref[pl.ds(0, nb), :] = km


def _select_topk(gate, valid, axis):
    idx = lax.broadcasted_iota(jnp.int32, gate.shape, axis).astype(F32)
    big = 1e9
    g = jnp.where(valid, gate, -jnp.inf)
    sel = jnp.zeros(gate.shape, jnp.bool_)
    for _ in range(MOBA_TOPK):
        mx = jnp.max(g, axis=axis, keepdims=True)
        is_max = (g == mx) & (mx > -jnp.inf)
        first = jnp.min(jnp.where(is_max, idx, big), axis=axis, keepdims=True)
        pick = idx == first
        sel = sel | pick
        g = jnp.where(pick, -jnp.inf, g)
    return jnp.where(sel, 0.0, NEG).astype(F32)


FAR_GROUP = 4


def _flash_kernel(q_ref, k_ref, v_ref, km_ref, bd_ref, bp_ref, bf_ref, o_ref,
                  kb_ref, vt_ref, qs_ref, sel_ref, m_ref, l_ref, acc_ref, sa_ref, sb_ref, *, nb):
    i = pl.program_id(1)
    T = MOBA_BLOCK

    @pl.when(i == 0)
    def _():
        def convert(n, carry):
            r0 = pl.multiple_of(n * T, T)
            kb_ref[n] = k_ref[pl.ds(r0, T), :].astype(BF16)
            vt_ref[n] = v_ref[pl.ds(r0, T), :].T.astype(BF16)
            return carry

        lax.fori_loop(0, nb, convert, 0)

    q = q_ref[...]
    qs_ref[...] = (q * (HD_ATTN ** -0.5)).astype(BF16)
    gate_t = lax.dot_general(km_ref[...], q, (((1,), (1,)), ((), ())),
                             precision=HIGHEST, preferred_element_type=F32)
    blk = lax.broadcasted_iota(jnp.int32, gate_t.shape, 0)
    sel_ref[...] = _select_topk(gate_t, blk < i, 0)
    m_ref[...] = jnp.full_like(m_ref, NEG)
    l_ref[...] = jnp.zeros_like(l_ref)
    acc_ref[...] = jnp.zeros_like(acc_ref)

    def raw_scores(j):
        return lax.dot_general(kb_ref[j], qs_ref[...], (((1,), (1,)), ((), ())), preferred_element_type=F32)

    def absorb(js, ss):
        m_old = m_ref[...]
        m_new = m_old
        for s in ss:
            m_new = jnp.maximum(m_new, jnp.max(s, axis=0, keepdims=True))
        alpha = jnp.exp(m_old - m_new)
        l_new = alpha * l_ref[...]
        acc = alpha * acc_ref[...]
        for j, s in zip(js, ss):
            p = jnp.exp(s - m_new)
            l_new = l_new + jnp.sum(p, axis=0, keepdims=True)
            acc = acc + jnp.dot(vt_ref[j], p.astype(BF16), preferred_element_type=F32)
        l_ref[...] = l_new
        acc_ref[...] = acc
        m_ref[...] = m_new

    def update(blocks):
        absorb([j for j, _ in blocks], [raw_scores(j) + extra for j, extra in blocks])

    def far_term(j):
        return bf_ref[...] + sel_ref[pl.ds(j, 1), :]

    n_far = jnp.maximum(i - 1, 0)
    n_groups = n_far // FAR_GROUP

    def fill(g, s_ref):
        for b in range(FAR_GROUP):
            s_ref[b] = raw_scores(g * FAR_GROUP + b)

    def drain(g, s_ref):
        js = [g * FAR_GROUP + b for b in range(FAR_GROUP)]
        absorb(js, [s_ref[b] + far_term(j) for b, j in enumerate(js)])

    @pl.when(n_groups > 0)
    def _():
        fill(0, sa_ref)

    def far_pair(gp, carry):
        g0 = 2 * gp
        fill(jnp.minimum(g0 + 1, n_groups - 1), sb_ref)
        drain(g0, sa_ref)
        fill(jnp.minimum(g0 + 2, n_groups - 1), sa_ref)
        drain(g0 + 1, sb_ref)
        return carry

    lax.fori_loop(0, n_groups // 2, far_pair, 0)

    @pl.when(n_groups % 2 == 1)
    def _():
        drain(n_groups - 1, sa_ref)

    n_left = n_far - n_groups * FAR_GROUP
    tail = []
    for b in range(FAR_GROUP - 1):
        j = jnp.minimum(n_groups * FAR_GROUP + b, i)
        tail.append((j, far_term(j) + jnp.where(b < n_left, 0.0, NEG)))
    j_prev = jnp.maximum(i - 1, 0)
    tail.append((j_prev, bp_ref[...] + (sel_ref[pl.ds(j_prev, 1), :] + jnp.where(i >= 1, 0.0, NEG))))
    tail.append((i, bd_ref[...]))
    update(tail)

    o_ref[...] = (acc_ref[...] / l_ref[...]).T


def moba_prompt(h2, k2, v2, rel_bias, n_heads, q_col):
    L = h2.shape[0]
    nb = L // MOBA_BLOCK
    assert L % MOBA_BLOCK == 0
    nbp = -(-nb // SUBLANES) * SUBLANES
    T = MOBA_BLOCK
    kmean = pl.pallas_call(
        functools.partial(_kmean_kernel, nb=nb),
        grid=(n_heads,),
        in_specs=[pl.BlockSpec((L, HD_ATTN), lambda h: (0, h))],
        out_specs=pl.BlockSpec((None, nbp, HD_ATTN), lambda h: (h, 0, 0)),
        out_shape=jax.ShapeDtypeStruct((n_heads, nbp, HD_ATTN), F32),
        compiler_params=_params(("parallel",)),
    )(k2)

    ki = jnp.arange(T)[:, None]
    qi = jnp.arange(T)[None, :]
    bias_diag = _bias_of_dist(rel_bias, qi - ki)
    bias_prev = _bias_of_dist(rel_bias, qi - ki + T)
    bias_far = jnp.broadcast_to(rel_bias[NUM_BUCKETS - 1][:, None, None], (n_heads, 1, T))

    return pl.pallas_call(
        functools.partial(_flash_kernel, nb=nb),
        grid=(n_heads, nb),
        in_specs=[
            pl.BlockSpec((T, HD_ATTN), lambda h, i: (i, q_col + h)),
            pl.BlockSpec((L, HD_ATTN), lambda h, i: (0, h)),
            pl.BlockSpec((L, HD_ATTN), lambda h, i: (0, h)),
            pl.BlockSpec((None, nbp, HD_ATTN), lambda h, i: (h, 0, 0)),
            pl.BlockSpec((None, T, T), lambda h, i: (h, 0, 0)),
            pl.BlockSpec((None, T, T), lambda h, i: (h, 0, 0)),
            pl.BlockSpec((None, 1, T), lambda h, i: (h, 0, 0)),
        ],
        out_specs=pl.BlockSpec((T, HD_ATTN), lambda h, i: (i, h)),
        out_shape=jax.ShapeDtypeStruct((L, n_heads * HD_ATTN), F32),
        scratch_shapes=[
            pltpu.VMEM((nb, T, HD_ATTN), BF16),
            pltpu.VMEM((nb, HD_ATTN, T), BF16),
            pltpu.VMEM((T, HD_ATTN), BF16),
            pltpu.VMEM((nbp, T), F32),
            pltpu.VMEM((1, T), F32),
            pltpu.VMEM((1, T), F32),
            pltpu.VMEM((HD_ATTN, T), F32),
            pltpu.VMEM((FAR_GROUP, T, T), F32),
            pltpu.VMEM((FAR_GROUP, T, T), F32),
        ],
        compiler_params=_params(("parallel", "arbitrary")),
    )(h2, k2, v2, kmean, bias_diag, bias_prev, bias_far)


QPAD = 8
SAMPLE_BLOCKS_PER_STEP = 8


def _head_rows(ref, h, n_heads):
    return ref[pl.ds(h, ref.shape[0] // n_heads, stride=n_heads), :]


def _sample_p1_kernel(pt_ref, *refs, bps, n_heads):
    n = pl.program_id(1)
    page_refs = refs[:2 * bps]
    w_ref, lg_ref, gate_ref = refs[2 * bps:]
    w = w_ref[...].astype(BF16)
    P = page_refs[0].shape[0] // n_heads
    C = w.shape[1]
    col_head = lax.broadcasted_iota(jnp.int32, (P, C), 1) // QPAD
    for blk in range(bps):
        gsum = jnp.zeros((1, C), F32)
        for half in range(2):
            ref = page_refs[2 * blk + half]
            s = jnp.zeros((P, C), F32)
            for h in range(n_heads):
                sh = jnp.dot(_head_rows(ref, h, n_heads).astype(BF16), w, preferred_element_type=F32)
                s = jnp.where(col_head == h, sh, s)
            gsum = gsum + jnp.sum(s, axis=0, keepdims=True)
            lg_ref[blk, pl.ds(half * P, P), :] = s * (HD_ATTN ** -0.5)
        gate_ref[pl.ds(n * bps + blk, 1), :] = gsum * (1.0 / MOBA_BLOCK)


def _sample_p2_kernel(lg_ref, gate_ref, kn_ref, w_ref, bl_ref, bf_ref, bo_ref, p_ref, pn_ref, *, nbp, n_heads):
    gate = gate_ref[...]
    selm = _select_topk(gate, jnp.ones(gate.shape, jnp.bool_), 0)
    s = lg_ref[...] + selm[:, None, :]
    blk = lax.broadcasted_iota(jnp.int32, (nbp, 1, 1), 0)
    s = s + jnp.where(blk == nbp - 1, bl_ref[...][None], bf_ref[...][None])
    w = w_ref[...].astype(BF16)
    col_head = lax.broadcasted_iota(jnp.int32, (QPAD, w.shape[1]), 1) // QPAD
    sn = jnp.zeros((QPAD, w.shape[1]), F32)
    for h in range(n_heads):
        sn = jnp.where(col_head == h, jnp.dot(_head_rows(kn_ref, h, n_heads).astype(BF16), w,
                                              preferred_element_type=F32), sn)
    sn = sn * (HD_ATTN ** -0.5) + bo_ref[...]
    m = jnp.maximum(jnp.max(jnp.max(s, axis=1), axis=0, keepdims=True), jnp.max(sn, axis=0, keepdims=True))
    p = jnp.exp(s - m[None])
    pn = jnp.exp(sn - m)
    den = jnp.sum(jnp.sum(p, axis=1), axis=0, keepdims=True) + jnp.sum(pn, axis=0, keepdims=True)
    inv = 1.0 / den
    p = p * inv[None]
    for blk in range(nbp):
        p_ref[blk] = p[blk].T
    pn_ref[...] = pn * inv


def _sample_p3_kernel(pt_ref, *refs, n_heads, bps):
    n = pl.program_id(1)
    page_refs = refs[:2 * bps]
    p_ref, pn_ref, vn_ref, o_ref, acc_ref = refs[2 * bps:]
    P = page_refs[0].shape[0] // n_heads

    head_rows = lambda h: pl.ds(h * QPAD, QPAD)

    @pl.when(n == 0)
    def _():
        pn = pn_ref[...].astype(BF16)
        for h in range(n_heads):
            full = lax.dot_general(pn, _head_rows(vn_ref, h, n_heads).astype(BF16), (((0,), (0,)), ((), ())),
                                   preferred_element_type=F32)
            acc_ref[h] = full[h * QPAD:(h + 1) * QPAD]

    for h in range(n_heads):
        acc = acc_ref[h]
        for blk in range(bps):
            for half in range(2):
                p = p_ref[blk, head_rows(h), pl.ds(half * P, P)].astype(BF16)
                acc = acc + jnp.dot(p, _head_rows(page_refs[2 * blk + half], h, n_heads).astype(BF16),
                                    preferred_element_type=F32)
        acc_ref[h] = acc

    @pl.when(n == pl.num_programs(1) - 1)
    def _():
        for h in range(n_heads):
            o_ref[:, pl.ds(h * HD_ATTN, HD_ATTN)] = acc_ref[h]


def moba_sample(q, k_new, v_new, cache_k, cache_v, layer, page_table, rel_bias):
    B, L, H, HD = q.shape
    page = cache_k.shape[2]
    n_pages = page_table.shape[1]
    past = n_pages * page
    assert MOBA_BLOCK == 2 * page and past % MOBA_BLOCK == 0 and L <= QPAD
    nbp = past // MOBA_BLOCK
    C = H * QPAD
    CH = H * HD
    qp = jnp.pad(q, ((0, 0), (0, QPAD - L), (0, 0), (0, 0)))
    wq = qp.transpose(0, 3, 2, 1).reshape(B, HD, C)

    qpos = past + jnp.arange(QPAD)
    kl = (nbp - 1) * MOBA_BLOCK + jnp.arange(MOBA_BLOCK)
    bias_last = _bias_of_dist(rel_bias, qpos[None, :] - kl[:, None])
    bias_last = jnp.moveaxis(bias_last, 0, 1).reshape(MOBA_BLOCK, C)
    bias_far = jnp.repeat(rel_bias[NUM_BUCKETS - 1], QPAD).reshape(1, C)
    kown = past + jnp.arange(QPAD)
    dist_own = qpos[None, :] - kown[:, None]
    ok = (jnp.arange(QPAD)[:, None] < L) & (jnp.arange(QPAD)[None, :] < L)
    bias_own = _bias_of_dist(rel_bias, jnp.where(ok, dist_own, -1))
    bias_own = jnp.moveaxis(bias_own, 0, 1).reshape(QPAD, C)

    rows_ph = lambda t: t.reshape(t.shape[:-3] + (t.shape[-3] * H, HD))
    knp = rows_ph(jnp.pad(k_new, ((0, 0), (0, QPAD - L), (0, 0), (0, 0))))
    vnp = rows_ph(jnp.pad(v_new, ((0, 0), (0, QPAD - L), (0, 0), (0, 0))))
    cache_k, cache_v = rows_ph(cache_k), rows_ph(cache_v)

    bps = _pick(nbp, SAMPLE_BLOCKS_PER_STEP)
    page_specs = [pl.BlockSpec((None, None, page * H, HD),
                               functools.partial(lambda b, n, pt, j: (layer, pt[b, 2 * bps * n + j], 0, 0), j=j))
                  for j in range(2 * bps)]
    logits, gate = pl.pallas_call(
        functools.partial(_sample_p1_kernel, bps=bps, n_heads=H),
        grid_spec=pltpu.PrefetchScalarGridSpec(
            num_scalar_prefetch=1,
            grid=(B, nbp // bps),
            in_specs=page_specs + [pl.BlockSpec((None, HD, C), lambda b, n, pt: (b, 0, 0))],
            out_specs=[pl.BlockSpec((None, bps, MOBA_BLOCK, C), lambda b, n, pt: (b, n, 0, 0)),
                       pl.BlockSpec((None, nbp, C), lambda b, n, pt: (b, 0, 0))],
        ),
        out_shape=[jax.ShapeDtypeStruct((B, nbp, MOBA_BLOCK, C), F32),
                   jax.ShapeDtypeStruct((B, nbp, C), F32)],
        compiler_params=_params(("parallel", "arbitrary")),
    )(page_table, *([cache_k] * (2 * bps)), wq)

    p_past, p_new = pl.pallas_call(
        functools.partial(_sample_p2_kernel, nbp=nbp, n_heads=H),
        grid=(B,),
        in_specs=[
            pl.BlockSpec((None, nbp, MOBA_BLOCK, C), lambda b: (b, 0, 0, 0)),
            pl.BlockSpec((None, nbp, C), lambda b: (b, 0, 0)),
            pl.BlockSpec((None, QPAD * H, HD), lambda b: (b, 0, 0)),
            pl.BlockSpec((None, HD, C), lambda b: (b, 0, 0)),
            pl.BlockSpec((MOBA_BLOCK, C), lambda b: (0, 0)),
            pl.BlockSpec((1, C), lambda b: (0, 0)),
            pl.BlockSpec((QPAD, C), lambda b: (0, 0)),
        ],
        out_specs=[pl.BlockSpec((None, nbp, C, MOBA_BLOCK), lambda b: (b, 0, 0, 0)),
                   pl.BlockSpec((None, QPAD, C), lambda b: (b, 0, 0))],
        out_shape=[jax.ShapeDtypeStruct((B, nbp, C, MOBA_BLOCK), F32),
                   jax.ShapeDtypeStruct((B, QPAD, C), F32)],
        compiler_params=_params(("parallel",)),
    )(logits, gate, knp, wq, bias_last, bias_far, bias_own)

    o = pl.pallas_call(
        functools.partial(_sample_p3_kernel, n_heads=H, bps=bps),
        grid_spec=pltpu.PrefetchScalarGridSpec(
            num_scalar_prefetch=1,
            grid=(B, nbp // bps),
            in_specs=page_specs + [
                pl.BlockSpec((None, bps, C, MOBA_BLOCK), lambda b, n, pt: (b, n, 0, 0)),
                pl.BlockSpec((None, QPAD, C), lambda b, n, pt: (b, 0, 0)),
                pl.BlockSpec((None, QPAD * H, HD), lambda b, n, pt: (b, 0, 0))],
            out_specs=pl.BlockSpec((None, QPAD, CH), lambda b, n, pt: (b, 0, 0)),
            scratch_shapes=[pltpu.VMEM((H, QPAD, HD), F32)],
        ),
        out_shape=jax.ShapeDtypeStruct((B, QPAD, CH), F32),
        compiler_params=_params(("parallel", "arbitrary")),
    )(page_table, *([cache_v] * (2 * bps)), p_past, p_new, vnp)
    return o[:, :L]


def _head_minor(x, n_heads, axis=-1):
    axis = axis % x.ndim
    shp = x.shape
    x = x.reshape(shp[:axis] + (n_heads, shp[axis] // n_heads) + shp[axis + 1:])
    return jnp.swapaxes(x, axis, axis + 1).reshape(shp)


def _head_sum(x, n_heads):
    nt = x.shape[1] // LANES
    t = x[:, 0:LANES]
    for c in range(1, nt):
        t = t + x[:, c * LANES:(c + 1) * LANES]
    shift = n_heads
    while shift < LANES:
        t = t + pltpu.roll(t, shift, axis=1)
        shift *= 2
    return jnp.concatenate([t] * nt, axis=1)


def _rwkv_prep_kernel(k_ref, a_ref, kk_ref, ka_ref, k2_ref, na_ref, nb_ref, *, n_heads):
    k = k_ref[...]
    a = a_ref[...]
    kk = k * kk_ref[...]
    nrm = jnp.maximum(jnp.sqrt(_head_sum(kk * kk, n_heads)), 1e-12)
    kk = kk / nrm
    k2_ref[...] = k * (1.0 + (a - 1.0) * ka_ref[...])
    na_ref[...] = -kk
    nb_ref[...] = kk * a


def rwkv_prep(k, a, k_k, k_a, n_heads, *, tm=256):
    M, D = k.shape
    tm = _pick(M, tm)
    blk = pl.BlockSpec((tm, D), lambda i: (i, 0))
    row = pl.BlockSpec((1, D), lambda i: (0, 0))
    return pl.pallas_call(
        functools.partial(_rwkv_prep_kernel, n_heads=n_heads),
        grid=(M // tm,),
        in_specs=[blk, blk, row, row],
        out_specs=[blk, blk, blk],
        out_shape=[jax.ShapeDtypeStruct((M, D), F32)] * 3,
        compiler_params=_params(("parallel",)),
    )(k, a, k_k.reshape(1, D), k_a.reshape(1, D))


def _rwkv_post_kernel(y_ref, r_ref, k2_ref, v_ref, g_ref, rk_ref, lg_ref, lb_ref, o_ref, *, n_heads):
    y = y_ref[...]
    inv = 1.0 / HS_RWKV
    mu = _head_sum(y, n_heads) * inv
    yc = y - mu
    var = _head_sum(yc * yc, n_heads) * inv
    yn = yc * lax.rsqrt(var + GN_EPS) * lg_ref[...] + lb_ref[...]
    bonus = _head_sum(r_ref[...] * k2_ref[...] * rk_ref[...], n_heads) * v_ref[...]
    o_ref[...] = (yn + bonus) * g_ref[...]


def rwkv_post(y, r, k2, v, g, r_k, lnx_g, lnx_b, n_heads, *, tm=256):
    M, D = y.shape
    tm = _pick(M, tm)
    blk = pl.BlockSpec((tm, D), lambda i: (i, 0))
    row = pl.BlockSpec((1, D), lambda i: (0, 0))
    return pl.pallas_call(
        functools.partial(_rwkv_post_kernel, n_heads=n_heads),
        grid=(M // tm,),
        in_specs=[blk] * 5 + [row] * 3,
        out_specs=blk,
        out_shape=jax.ShapeDtypeStruct((M, D), F32),
        compiler_params=_params(("parallel",)),
    )(y, r, k2, v, g, r_k.reshape(1, D), lnx_g.reshape(1, D), lnx_b.reshape(1, D))


def _split3(x):
    hi = x.astype(BF16)
    r1 = x - hi.astype(F32)
    mid = r1.astype(BF16)
    lo = (r1 - mid.astype(F32)).astype(BF16)
    return hi, mid, lo


SCAN_GROUP = 16


def _scan_kernel(r_ref, d_ref, k_ref, a_ref, b_ref, v_ref, s0_ref, y_ref, so_ref, cola_ref, colb_ref, s_ref, vy_ref,
                 *, ng, nh, nvh, n_valid):
    c = pl.program_id(1)
    G = SCAN_GROUP
    klo = LANES // nh
    nt = HS_RWKV // klo

    @pl.when(c == 0)
    def _():
        s_ref[...] = s0_ref[...]

    rr = lax.broadcasted_iota(jnp.int32, (LANES, klo * LANES), 0)
    cc = lax.broadcasted_iota(jnp.int32, (LANES, klo * LANES), 1)
    spread = ((rr // nh == cc // LANES) & (rr % nh == cc % nh)).astype(BF16)
    spread3 = jnp.concatenate([spread, spread, spread], axis=0)
    ops = (r_ref, d_ref, k_ref, a_ref, b_ref)

    def build(gi, col_ref):
        rows = pl.ds(pl.multiple_of(gi * G, G), G)
        pieces = []
        for ref in ops:
            hi, mid, lo = _split3(ref[rows, :])
            for jt in range(nt):
                ls = slice(jt * LANES, (jt + 1) * LANES)
                pieces.append(jnp.concatenate([hi[:, ls], mid[:, ls], lo[:, ls]], axis=1))
        out = jnp.dot(jnp.concatenate(pieces, axis=0), spread3, preferred_element_type=F32)
        for o in range(len(ops)):
            for jt in range(nt):
                src = (o * nt + jt) * G
                for kl in range(klo):
                    dst = (o * HS_RWKV + jt * klo + kl) * G
                    col_ref[pl.ds(dst, G), :] = out[src:src + G, kl * LANES:(kl + 1) * LANES]

    def consume(gi, col_ref, ntok):
        rows = pl.ds(pl.multiple_of(gi * G, G), G)
        vy_ref[...] = v_ref[rows, :]
        for tt in range(ntok):
            tile = lambda o: col_ref[pl.ds(o * HS_RWKV * G + tt, HS_RWKV, stride=G), :]
            r_c, d_c, k_c, a_c, b_c = (tile(o) for o in range(5))
            for vh in range(nvh):
                lanes = pl.ds(vh * LANES, LANES)
                S = s_ref[vh]
                sa = jnp.sum(S * a_c, axis=0, keepdims=True)
                S = S * d_c + sa * b_c + vy_ref[pl.ds(tt, 1), lanes] * k_c
                s_ref[vh] = S
                vy_ref[pl.ds(tt, 1), lanes] = jnp.sum(S * r_c, axis=0, keepdims=True)
        y_ref[rows, :] = vy_ref[...]

    build(0, cola_ref)
    if ng % 2:
        assert ng == 1
        consume(0, cola_ref, n_valid)
    else:
        assert n_valid == G
        def pair(gp, carry):
            g0 = 2 * gp
            build(g0 + 1, colb_ref)
            consume(g0, cola_ref, G)
            build(jnp.minimum(g0 + 2, ng - 1), cola_ref)
            consume(g0 + 1, colb_ref, G)
            return carry

        lax.fori_loop(0, ng // 2, pair, 0)

    @pl.when(c == pl.num_programs(1) - 1)
    def _():
        so_ref[...] = s_ref[...]


def rwkv_scan(r, d, k, a, b, v, state, n_heads, *, tc=256):
    B, L, D = r.shape
    H = n_heads
    G = SCAN_GROUP
    vlo = LANES // H
    nvh = HS_RWKV // vlo
    assert H * vlo == LANES and nvh * vlo == HS_RWKV
    Lp = -(-L // G) * G
    n_valid = G
    if Lp != L:
        assert Lp == G
        n_valid = L
        r, d, k, a, b, v = (jnp.pad(t, ((0, 0), (0, Lp - L), (0, 0))) for t in (r, d, k, a, b, v))
    tc = _pick(Lp, tc)
    ng = tc // G
    s0 = state.reshape(B, H, nvh, vlo, HS_RWKV).transpose(0, 2, 4, 3, 1).reshape(B, nvh, HS_RWKV, LANES)
    xspec = pl.BlockSpec((None, tc, D), lambda bb, c: (bb, c, 0))
    sspec = pl.BlockSpec((None, nvh, HS_RWKV, LANES), lambda bb, c: (bb, 0, 0, 0))
    y, s1 = pl.pallas_call(
        functools.partial(_scan_kernel, ng=ng, nh=H, nvh=nvh, n_valid=n_valid),
        grid=(B, Lp // tc),
        in_specs=[xspec] * 6 + [sspec],
        out_specs=[xspec, sspec],
        out_shape=[jax.ShapeDtypeStruct((B, Lp, D), F32),
                   jax.ShapeDtypeStruct((B, nvh, HS_RWKV, LANES), F32)],
        scratch_shapes=[pltpu.VMEM((5 * HS_RWKV * G, LANES), F32),
                        pltpu.VMEM((5 * HS_RWKV * G, LANES), F32),
                        pltpu.VMEM((nvh, HS_RWKV, LANES), F32),
                        pltpu.VMEM((G, D), F32)],
        compiler_params=_params(("parallel", "arbitrary")),
    )(r, d, k, a, b, v, s0)
    s1 = s1.reshape(B, nvh, HS_RWKV, vlo, H).transpose(0, 4, 1, 3, 2).reshape(B, H, HS_RWKV, HS_RWKV)
    return y[:, :L], s1


def _conv_attn_layer(x, conv_prev, attn_fn, w_in, conv_w, conv_b, cln_g, cln_b, w_out, ln_g, ln_b, alpha,
                     c_conv, c_attn):
    B, L, D = x.shape
    x2 = x.reshape(B * L, D)
    n_hq = 2 * c_conv + c_attn
    h = matmul(x2, w_in, cols=(0, n_hq), tm=1024).reshape(B, L, -1)
    k = matmul(x2, w_in, cols=(n_hq, c_attn), tm=1024, tn=c_attn).reshape(B, L, c_attn)
    v = matmul(x2, w_in, cols=(n_hq + c_attn, c_attn), tm=1024, tn=c_attn).reshape(B, L, c_attn)
    c, cstate = conv_module(h, conv_prev, conv_w, conv_b, cln_g, cln_b, c_conv)
    o = attn_fn(h, k, v)
    xn = rowmm_ln([c.reshape(B * L, c_conv), o.reshape(B * L, c_attn)], w_out, x2, ln_g, ln_b, alpha).reshape(B, L, D)
    return xn, k, v, cstate


def _rwkv_layer(x, shift_prev, wkv_prev, p, ln_g, ln_b, alpha, n_heads):
    B, L, D = x.shape
    if B == 1:
        x_prev = shift_prev
    else:
        x_prev = jnp.concatenate([shift_prev[:, None], x[:, :-1]], axis=1).reshape(B * L, D)
    x2 = x.reshape(B * L, D)
    mixrow = lambda i: p["mix"][i].reshape(1, D)
    proj = lambda i, w, **kw: matmul(x2, w, xprev=x_prev, mixrow=mixrow(i), tn=2048, **kw)
    r = proj(0, p["wr"])
    lw = proj(1, p["w1"], epilogue="tanh")
    d = matmul(lw, p["w2"], bias=p["w0"].reshape(1, D), epilogue="decay", tn=2048)
    k = proj(2, p["wk"])
    v = proj(3, p["wv"])
    la = proj(4, p["a1"])
    a = matmul(la, p["a2"], bias=p["a0"].reshape(1, D), epilogue="sigmoid", tn=2048)
    lg = proj(5, p["g1"], epilogue="sigmoid")
    g = matmul(lg, p["g2"], tn=2048)
    k2, na, nb = rwkv_prep(k, a, p["kk"], p["ka"], n_heads)
    sh = lambda t: t.reshape(B, L, D)
    y, s1 = rwkv_scan(sh(r), sh(d), sh(k2), sh(na), sh(nb), sh(v), wkv_prev, n_heads)
    yg = rwkv_post(y.reshape(B * L, D), r, k2, v, g, p["rk"], p["lnx_g"], p["lnx_b"], n_heads)
    xn = rowmm_ln([yg], p["wo"], x2, ln_g, ln_b, alpha).reshape(B, L, D)
    return xn, s1, x[:, -1]


def kernel(x_prompt, x_sample, cache_k, cache_v, page_table, state_conv, state_wkv, state_shift, rel_bias, ab_w_in, ab_conv_w, ab_conv_b, ab_ln_g, ab_ln_b, ab_w_out, rw_mix, rw_wr, rw_wk, rw_wv, rw_wo, rw_w0, rw_w1, rw_w2, rw_a0, rw_a1, rw_a2, rw_g1, rw_g2, rw_kk, rw_ka, rw_rk, rw_lnx_g, rw_lnx_b, ln_g, ln_b, mlp_w1, mlp_w2):
    B, L, D = x_prompt.shape
    DB, DL, _ = x_sample.shape
    depth = ln_g.shape[0]
    alpha = (2 * depth) ** 0.25
    c_conv = ab_conv_w.shape[2]
    n_heads_attn = cache_k.shape[3]
    c_attn = n_heads_attn * HD_ATTN
    n_heads_rwkv = D // HS_RWKV
    assert B == 1
    bf = lambda w: w.astype(BF16)
    row = lambda a: a.reshape(1, -1)
    q_col = 2 * c_conv // LANES
    k_col = q_col + c_attn // LANES
    v_col = k_col + c_attn // LANES

    xp, xs = x_prompt, x_sample
    outs = {n: [] for n in ("kp", "vp", "ks", "vs", "cp", "cs", "wp", "ws", "hp", "hs")}
    for layer in range(depth):
        i = layer // 2
        g0, b0, g1, b1 = row(ln_g[layer, 0]), row(ln_b[layer, 0]), row(ln_g[layer, 1]), row(ln_b[layer, 1])
        if layer % 2 == 0:
            ab = (bf(ab_w_in[i]), ab_conv_w[i], ab_conv_b[i], ab_ln_g[i], ab_ln_b[i], bf(ab_w_out[i]), g0, b0, alpha,
                  c_conv, c_attn)
            flat = lambda t: t.reshape(L, -1)
            attn_p = lambda h, k, v: moba_prompt(flat(h), flat(k), flat(v), rel_bias, n_heads_attn, q_col)[None]
            xp, kp, vp, cp = _conv_attn_layer(xp, jnp.zeros((B, CONV_W - 1, c_conv), F32), attn_p, *ab)

            def attn_s(h, k, v):
                hd = lambda t: t.reshape(DB, DL, n_heads_attn, HD_ATTN)
                return moba_sample(hd(h[..., 2 * c_conv:]), hd(k), hd(v), cache_k, cache_v, i, page_table, rel_bias)

            xs, ks, vs, cs = _conv_attn_layer(xs, state_conv[i], attn_s, *ab)
            hd4 = lambda t: t.reshape(t.shape[0], t.shape[1], n_heads_attn, HD_ATTN)
            outs["kp"].append(hd4(kp)); outs["vp"].append(hd4(vp))
            outs["ks"].append(hd4(ks)); outs["vs"].append(hd4(vs))
            outs["cp"].append(cp); outs["cs"].append(cs)
        else:
            hm = functools.partial(_head_minor, n_heads=n_heads_rwkv)
            p = dict(mix=rw_mix[i], wr=bf(hm(rw_wr[i])), wk=bf(hm(rw_wk[i])), wv=bf(hm(rw_wv[i])),
                     wo=bf(hm(rw_wo[i], axis=0)), w0=hm(rw_w0[i]), w1=bf(rw_w1[i]), w2=bf(hm(rw_w2[i])),
                     a0=hm(rw_a0[i]), a1=bf(rw_a1[i]), a2=bf(hm(rw_a2[i])), g1=bf(rw_g1[i]), g2=bf(hm(rw_g2[i])),
                     kk=hm(rw_kk[i]), ka=hm(rw_ka[i]), rk=hm(rw_rk[i].reshape(-1)),
                     lnx_g=hm(rw_lnx_g[i]), lnx_b=hm(rw_lnx_b[i]))
            xp, wp, hp = _rwkv_layer(xp, jnp.zeros((B, D), F32),
                                     jnp.zeros((B, n_heads_rwkv, HS_RWKV, HS_RWKV), F32), p, g0, b0, alpha,
                                     n_heads_rwkv)
            xs, ws, hs = _rwkv_layer(xs, state_shift[i], state_wkv[i], p, g0, b0, alpha, n_heads_rwkv)
            outs["wp"].append(wp); outs["ws"].append(ws); outs["hp"].append(hp); outs["hs"].append(hs)
        w1b, w2b = bf(mlp_w1[layer]), bf(mlp_w2[layer])
        xp = mlp_ln(xp.reshape(B * L, D), w1b, w2b, g1, b1, alpha).reshape(B, L, D)
        xs = mlp_ln(xs.reshape(DB * DL, D), w1b, w2b, g1, b1, alpha).reshape(DB, DL, D)
    st = lambda n: jnp.stack(outs[n])
    return (xp, xs, st("kp"), st("vp"), st("ks"), st("vs"), st("cp"), st("cs"),
            st("wp"), st("ws"), st("hp"), st("hs"))
```
